```python
import math
import jax
import jax.numpy as jnp
from jax import lax
import numpy as np

D_MODEL = 1024
BATCH = 16
SEQ = 4096
DEPTH = 2

N_MIXERS = 2
N_CONV_LAYERS = (DEPTH + 1) // 2
N_GDN_LAYERS = DEPTH // 2
D_FF = 4 * D_MODEL
CONV_WIDTH = 31
GDN_HEADS = 8
GDN_HEAD_K = D_MODEL // GDN_HEADS
GDN_HEAD_V = D_MODEL // GDN_HEADS
GDN_KEY_DIM = GDN_HEADS * GDN_HEAD_K
GDN_VAL_DIM = GDN_HEADS * GDN_HEAD_V
GDN_QKV_DIM = 2 * GDN_KEY_DIM + GDN_VAL_DIM
GDN_IN_DIM = GDN_QKV_DIM + GDN_VAL_DIM + 2 * GDN_HEADS
SHORT_CONV_WIDTH = 4
CHUNK = 64
NORM_EPS = 1e-6

kernel_name = 'hybrid_conformer_conv_gated_deltanet_trunk'


def rms_norm(x, g, eps=NORM_EPS):
    xf = x.astype(jnp.float32)
    y = xf * lax.rsqrt(jnp.mean(xf * xf, axis=-1, keepdims=True) + eps)
    return (y * g.astype(jnp.float32)).astype(x.dtype)


def layer_norm(x, g, b, eps=NORM_EPS):
    xf = x.astype(jnp.float32)
    mu = jnp.mean(xf, axis=-1, keepdims=True)
    xc = xf - mu
    y = xc * lax.rsqrt(jnp.mean(xc * xc, axis=-1, keepdims=True) + eps)
    return (y * g.astype(jnp.float32) + b.astype(jnp.float32)).astype(x.dtype)


def l2norm(x, eps=1e-6):
    xf = x.astype(jnp.float32)
    return xf * lax.rsqrt(jnp.sum(xf * xf, axis=-1, keepdims=True) + eps)


def causal_depthwise_conv(x, w):
    K, C = w.shape
    return lax.conv_general_dilated(
        x, w[:, None, :].astype(x.dtype), window_strides=(1,),
        padding=[(K - 1, 0)], dimension_numbers=('NWC', 'WIO', 'NWC'),
        feature_group_count=C)


def conformer_conv(h, w_pw1, b_pw1, w_dw, b_dw, ln_g, ln_b, w_pw2, b_pw2):
    u = h @ w_pw1 + b_pw1
    u = jax.nn.glu(u, axis=-1)
    u = causal_depthwise_conv(u, w_dw) + b_dw
    u = jax.nn.silu(layer_norm(u, ln_g, ln_b))
    return u @ w_pw2 + b_pw2


def chunk_gated_delta_rule(q, k, v, g, beta):
    B, S, H, dk = q.shape
    dv = v.shape[-1]
    N = S // CHUNK
    q, k, v = [jnp.swapaxes(t, 1, 2).reshape(B, H, N, CHUNK, -1) for t in (q, k, v)]
    g, beta = [jnp.swapaxes(t, 1, 2).reshape(B, H, N, CHUNK) for t in (g, beta)]
    g = jnp.cumsum(g, axis=-1)
    idx = jnp.arange(CHUNK)
    causal = idx[:, None] >= idx[None, :]
    strict = idx[:, None] > idx[None, :]
    decay = jnp.exp(jnp.where(causal, g[..., :, None] - g[..., None, :], -jnp.inf))
    k_beta = k * beta[..., None]
    kk = jnp.einsum('bhnid,bhnjd->bhnij', k_beta, k) * decay
    m = jnp.where(strict, kk, 0.0) + jnp.eye(CHUNK, dtype=jnp.float32)
    rhs = jnp.concatenate([v * beta[..., None], k_beta * jnp.exp(g)[..., None]], axis=-1)
    sol = lax.linalg.triangular_solve(m, rhs, left_side=True, lower=True, unit_diagonal=True)
    u = sol[..., :dv]
    w = sol[..., dv:]
    qk = jnp.einsum('bhnid,bhnjd->bhnij', q, k) * decay

    def step(state, xs):
        q_c, k_c, u_c, w_c, qk_c, g_c = xs
        v_new = u_c - jnp.einsum('bhcd,bhde->bhce', w_c, state)
        o = (jnp.einsum('bhcd,bhde->bhce', q_c * jnp.exp(g_c)[..., None], state)
             + jnp.einsum('bhij,bhje->bhie', qk_c, v_new))
        g_last = g_c[..., -1]
        state = (state * jnp.exp(g_last)[..., None, None]
                 + jnp.einsum('bhcd,bhce->bhde',
                              k_c * jnp.exp(g_last[..., None] - g_c)[..., None], v_new))
        return state, o

    xs = tuple(jnp.moveaxis(t, 2, 0) for t in (q, k, u, w, qk, g))
    state0 = jnp.zeros((B, H, dk, dv), jnp.float32)
    _, o = lax.scan(step, state0, xs)
    o = jnp.moveaxis(o, 0, 2).reshape(B, H, S, dv)
    return jnp.swapaxes(o, 1, 2)


def gated_deltanet(h, w_in, conv_w, a_log, dt_bias, norm_g, w_out):
    B, S, _ = h.shape
    proj = h @ w_in
    qkv = proj[..., :GDN_QKV_DIM]
    z = proj[..., GDN_QKV_DIM:GDN_QKV_DIM + GDN_VAL_DIM]
    a_raw = proj[..., GDN_QKV_DIM + GDN_VAL_DIM:GDN_QKV_DIM + GDN_VAL_DIM + GDN_HEADS]
    b_raw = proj[..., GDN_QKV_DIM + GDN_VAL_DIM + GDN_HEADS:]
    qkv = jax.nn.silu(causal_depthwise_conv(qkv, conv_w))
    q = qkv[..., :GDN_KEY_DIM].reshape(B, S, GDN_HEADS, GDN_HEAD_K)
    k = qkv[..., GDN_KEY_DIM:2 * GDN_KEY_DIM].reshape(B, S, GDN_HEADS, GDN_HEAD_K)
    v = qkv[..., 2 * GDN_KEY_DIM:].reshape(B, S, GDN_HEADS, GDN_HEAD_V).astype(jnp.float32)
    q = l2norm(q) * (GDN_HEAD_K ** -0.5)
    k = l2norm(k)
    beta = jax.nn.sigmoid(b_raw.astype(jnp.float32))
    g = -jnp.exp(a_log.astype(jnp.float32)) * jax.nn.softplus(
        a_raw.astype(jnp.float32) + dt_bias.astype(jnp.float32))
    o = chunk_gated_delta_rule(q, k, v, g, beta)
    zf = z.reshape(B, S, GDN_HEADS, GDN_HEAD_V).astype(jnp.float32)
    o = rms_norm(o, norm_g) * jax.nn.silu(zf)
    return o.reshape(B, S, GDN_VAL_DIM).astype(h.dtype) @ w_out


def sqrelu_mlp(h, w1, w2):
    return jnp.square(jax.nn.relu(h @ w1)) @ w2


def _normal(key, shape, fan_in):
    return jax.random.normal(key, shape, jnp.float32) * (fan_in ** -0.5)


def setup_inputs(seed: int = 0) -> dict:
    key = jax.random.key(seed)
    ks = jax.random.split(key, 24)
    D = D_MODEL
    Nc, Ng = N_CONV_LAYERS, N_GDN_LAYERS
    x = jax.random.normal(ks[0], (BATCH, SEQ, D), jnp.float32)
    norm_mix_g = 1.0 + 0.02 * jax.random.normal(ks[1], (DEPTH, D), jnp.float32)
    norm_ffn_g = 1.0 + 0.02 * jax.random.normal(ks[2], (DEPTH, D), jnp.float32)
    final_norm_g = 1.0 + 0.02 * jax.random.normal(ks[3], (D,), jnp.float32)
    cv_w_pw1 = _normal(ks[4], (Nc, D, 2 * D), D)
    cv_b_pw1 = 0.01 * jax.random.normal(ks[5], (Nc, 2 * D), jnp.float32)
    cv_w_dw = _normal(ks[6], (Nc, CONV_WIDTH, D), CONV_WIDTH)
    cv_b_dw = 0.01 * jax.random.normal(ks[7], (Nc, D), jnp.float32)
    cv_ln_g = 1.0 + 0.02 * jax.random.normal(ks[8], (Nc, D), jnp.float32)
    cv_ln_b = 0.01 * jax.random.normal(ks[9], (Nc, D), jnp.float32)
    cv_w_pw2 = _normal(ks[10], (Nc, D, D), D)
    cv_b_pw2 = 0.01 * jax.random.normal(ks[11], (Nc, D), jnp.float32)
    gdn_w_in = _normal(ks[12], (Ng, D, GDN_IN_DIM), D)
    gdn_conv_w = _normal(ks[13], (Ng, SHORT_CONV_WIDTH, GDN_QKV_DIM), SHORT_CONV_WIDTH)
    gdn_a_log = jnp.log(jax.random.uniform(ks[14], (Ng, GDN_HEADS), jnp.float32, 1.0, 16.0))
    dt = jnp.exp(jax.random.uniform(ks[15], (Ng, GDN_HEADS), jnp.float32,
                                    math.log(1e-3), math.log(1e-1)))
    gdn_dt_bias = dt + jnp.log(-jnp.expm1(-dt))
    gdn_norm_g = 1.0 + 0.02 * jax.random.normal(ks[16], (Ng, GDN_HEAD_V), jnp.float32)
    gdn_w_out = _normal(ks[17], (Ng, GDN_VAL_DIM, D), GDN_VAL_DIM)
    mlp_w1 = _normal(ks[18], (DEPTH, D, D_FF), D)
    mlp_w2 = _normal(ks[19], (DEPTH, D_FF, D), D_FF)
    return {'x': x, 'norm_mix_g': norm_mix_g, 'norm_ffn_g': norm_ffn_g,
            'final_norm_g': final_norm_g,
            'cv_w_pw1': cv_w_pw1, 'cv_b_pw1': cv_b_pw1, 'cv_w_dw': cv_w_dw,
            'cv_b_dw': cv_b_dw, 'cv_ln_g': cv_ln_g, 'cv_ln_b': cv_ln_b,
            'cv_w_pw2': cv_w_pw2, 'cv_b_pw2': cv_b_pw2,
            'gdn_w_in': gdn_w_in, 'gdn_conv_w': gdn_conv_w, 'gdn_a_log': gdn_a_log,
            'gdn_dt_bias': gdn_dt_bias, 'gdn_norm_g': gdn_norm_g, 'gdn_w_out': gdn_w_out,
            'mlp_w1': mlp_w1, 'mlp_w2': mlp_w2}


def reference(x, norm_mix_g, norm_ffn_g, final_norm_g,
              cv_w_pw1, cv_b_pw1, cv_w_dw, cv_b_dw, cv_ln_g, cv_ln_b, cv_w_pw2, cv_b_pw2,
              gdn_w_in, gdn_conv_w, gdn_a_log, gdn_dt_bias, gdn_norm_g, gdn_w_out,
              mlp_w1, mlp_w2):
    h = x
    for i in range(DEPTH):
        hn = rms_norm(h, norm_mix_g[i])
        j = i // N_MIXERS
        if i % N_MIXERS == 0:
            mix = conformer_conv(hn, cv_w_pw1[j], cv_b_pw1[j], cv_w_dw[j], cv_b_dw[j],
                                 cv_ln_g[j], cv_ln_b[j], cv_w_pw2[j], cv_b_pw2[j])
        else:
            mix = gated_deltanet(hn, gdn_w_in[j], gdn_conv_w[j], gdn_a_log[j],
                                 gdn_dt_bias[j], gdn_norm_g[j], gdn_w_out[j])
        h = h + mix
        h = h + sqrelu_mlp(rms_norm(h, norm_ffn_g[i]), mlp_w1[i], mlp_w2[i])
    return rms_norm(h, final_norm_g)
```

```python
import functools

import jax
import jax.numpy as jnp
from jax import lax
from jax.experimental import pallas as pl
from jax.experimental.pallas import tpu as pltpu

F32 = jnp.float32
BF16 = jnp.bfloat16
NORM_EPS = 1e-6
L2_EPS = 1e-6
CHUNK = 64
LANES = 128
SUBLANES = 8
VMEM_LIMIT_BYTES = 56 * 1024 * 1024

CONV_TILE = 512
MLP_TILE = 512
GDN_IN_TILE = 256
DELTA_TILE = 256
GDN_OUT_TILE = 512
CONV_HALO = 32
SHORT_HALO = 8
CONV_ROWS = 32
FF_CHUNK = 512


def _rms_norm(x, g):
    ms = jnp.mean(x * x, axis=-1, keepdims=True)
    return x * lax.rsqrt(ms + NORM_EPS) * g


def _silu(x):
    return x * jax.nn.sigmoid(x)


def _dot(a, b):
    return jnp.dot(a, b, preferred_element_type=F32)


def _dot_t(a, b):
    return lax.dot_general(a, b, (((1,), (1,)), ((), ())), preferred_element_type=F32)


def _tdot(a, b):
    return lax.dot_general(a, b, (((0,), (0,)), ((), ())), preferred_element_type=F32)


def _full(shape):
    return pl.BlockSpec(shape, lambda b, s: (0,) * len(shape))


def _tile(ts, d):
    return pl.BlockSpec((None, ts, d), lambda b, s: (b, s, 0))


def _params():
    return pltpu.CompilerParams(
        dimension_semantics=("arbitrary", "arbitrary"),
        vmem_limit_bytes=VMEM_LIMIT_BYTES)


def _conv_mixer_kernel(x_ref, g_ref, w1_ref, b1_ref, wdw_ref, bdw_ref, lng_ref, lnb_ref,
                       w2_ref, b2_ref, o_ref, ubuf, cvo, act):
    s = pl.program_id(1)
    ts, d = x_ref.shape
    slabs, taps = wdw_ref.shape[0], wdw_ref.shape[1]
    hn = _rms_norm(x_ref[...], g_ref[...]).astype(BF16)
    u = _dot(hn, w1_ref[...]) + b1_ref[...]
    glu = u[:, :d] * jax.nn.sigmoid(u[:, d:])

    @pl.when(s == 0)
    def _():
        ubuf[:, 0:CONV_HALO, :] = jnp.zeros((slabs, CONV_HALO, LANES), F32)

    for lt in range(slabs):
        ubuf[lt, CONV_HALO:CONV_HALO + ts, :] = glu[:, lt * LANES:(lt + 1) * LANES]
    base = CONV_HALO - (taps - 1)
    stride = CONV_ROWS // SUBLANES

    for lt in range(slabs):
        w = [wdw_ref[lt, k] for k in range(taps)]
        bias = bdw_ref[lt]

        def rows(c, carry, lt=lt, w=w, bias=bias):
            r0 = c * CONV_ROWS
            acc = [bias] * stride
            for m in range(taps + stride - 1):
                win = ubuf[lt, pl.ds(r0 + base + m, SUBLANES, stride=stride), :]
                for j in range(stride):
                    if 0 <= m - j < taps:
                        acc[j] = acc[j] + w[m - j] * win
            for j in range(stride):
                cvo[lt, pl.ds(r0 + j, SUBLANES, stride=stride), :] = acc[j]
            return carry

        lax.fori_loop(0, ts // CONV_ROWS, rows, 0)

    for lt in range(slabs):
        ubuf[lt, 0:CONV_HALO, :] = ubuf[lt, ts:ts + CONV_HALO, :]

    def norm_rows(c, carry):
        r0 = pl.multiple_of(c * CONV_ROWS, CONV_ROWS)
        xs = [cvo[lt, pl.ds(r0, CONV_ROWS), :] for lt in range(slabs)]
        tot = xs[0]
        for lt in range(1, slabs):
            tot = tot + xs[lt]
        mu = jnp.sum(tot, axis=-1, keepdims=True) * (1.0 / d)
        xc = [x - mu for x in xs]
        sq = xc[0] * xc[0]
        for lt in range(1, slabs):
            sq = sq + xc[lt] * xc[lt]
        inv = lax.rsqrt(jnp.sum(sq, axis=-1, keepdims=True) * (1.0 / d) + NORM_EPS)
        for lt in range(slabs):
            cols = slice(lt * LANES, (lt + 1) * LANES)
            y = xc[lt] * inv * lng_ref[:, cols] + lnb_ref[:, cols]
            act[pl.ds(r0, CONV_ROWS), cols] = _silu(y).astype(BF16)
        return carry

    lax.fori_loop(0, ts // CONV_ROWS, norm_rows, 0)
    o_ref[...] = x_ref[...] + _dot(act[...], w2_ref[...]) + b2_ref[...]


def _conv_mixer(h, g, w1, b1, wdw, bdw, lng, lnb, w2, b2):
    bsz, seq, d = h.shape
    ts = min(CONV_TILE, seq)
    taps = wdw.shape[0]
    slabs = d // LANES
    wdw_b = jnp.broadcast_to(wdw.reshape(taps, slabs, 1, LANES).transpose(1, 0, 2, 3),
                             (slabs, taps, SUBLANES, LANES))
    bdw_b = jnp.broadcast_to(bdw.reshape(slabs, 1, LANES), (slabs, SUBLANES, LANES))
    return pl.pallas_call(
        _conv_mixer_kernel,
        out_shape=jax.ShapeDtypeStruct(h.shape, F32),
        grid=(bsz, seq // ts),
        in_specs=[_tile(ts, d), _full((1, d)), _full((d, 2 * d)), _full((1, 2 * d)),
                  _full((slabs, taps, SUBLANES, LANES)), _full((slabs, SUBLANES, LANES)),
                  _full((1, d)), _full((1, d)), _full((d, d)), _full((1, d))],
        out_specs=_tile(ts, d),
        scratch_shapes=[pltpu.VMEM((slabs, CONV_HALO + ts, LANES), F32),
                        pltpu.VMEM((slabs, ts, LANES), F32),
                        pltpu.VMEM((ts, d), BF16)],
        compiler_params=_params(),
        name="conv_mixer",
    )(h, g.reshape(1, d), w1.astype(BF16), b1.reshape(1, 2 * d), wdw_b, bdw_b,
      lng.reshape(1, d), lnb.reshape(1, d), w2.astype(BF16), b2.reshape(1, d))


def _mlp_kernel(h_ref, g_ref, w1_ref, w2_ref, fg_ref, o_ref, *, final_norm):
    d_ff = w1_ref.shape[1]
    hn = _rms_norm(h_ref[...], g_ref[...]).astype(BF16)
    acc = None
    for j in range(d_ff // FF_CHUNK):
        cols = slice(j * FF_CHUNK, (j + 1) * FF_CHUNK)
        a = jnp.maximum(_dot(hn, w1_ref[:, cols]), 0.0)
        p = _dot((a * a).astype(BF16), w2_ref[cols, :])
        acc = p if acc is None else acc + p
    y = h_ref[...] + acc
    if final_norm:
        y = _rms_norm(y, fg_ref[...])
    o_ref[...] = y


def _mlp(h, g, w1, w2, final_g, final_norm):
    bsz, seq, d = h.shape
    d_ff = w1.shape[1]
    ts = min(MLP_TILE, seq)
    return pl.pallas_call(
        functools.partial(_mlp_kernel, final_norm=final_norm),
        out_shape=jax.ShapeDtypeStruct(h.shape, F32),
        grid=(bsz, seq // ts),
        in_specs=[_tile(ts, d), _full((1, d)), _full((d, d_ff)), _full((d_ff, d)), _full((1, d))],
        out_specs=_tile(ts, d),
        compiler_params=_params(),
        name="mlp_final" if final_norm else "mlp",
    )(h, g.reshape(1, d), w1.astype(BF16), w2.astype(BF16), final_g.reshape(1, d))


def _gdn_in_kernel(h_ref, g_ref, wq_ref, wab_ref, cw_ref, alog_ref, dtb_ref,
                   q_ref, k_ref, v_ref, z_ref, gt_ref, cbuf, *, heads):
    s = pl.program_id(1)
    ts, d = h_ref.shape
    kd = q_ref.shape[1]
    dk = kd // heads
    qkv_dim = cbuf.shape[1]
    taps = cw_ref.shape[0]
    hn = _rms_norm(h_ref[...], g_ref[...]).astype(BF16)
    proj = _dot(hn, wq_ref[...])
    z_ref[...] = proj[:, qkv_dim:]

    ab = _dot(hn, wab_ref[...])
    pre = ab + dtb_ref[...]
    softplus = jnp.maximum(pre, 0.0) + jnp.log1p(jnp.exp(-jnp.abs(pre)))
    lane = lax.broadcasted_iota(jnp.int32, ab.shape, 1)
    gt_ref[...] = jnp.where(lane < heads, -jnp.exp(alog_ref[...]) * softplus, jax.nn.sigmoid(ab))

    @pl.when(s == 0)
    def _():
        cbuf[0:SHORT_HALO, :] = jnp.zeros((SHORT_HALO, qkv_dim), F32)

    cbuf[SHORT_HALO:SHORT_HALO + ts, :] = proj[:, :qkv_dim]
    base = SHORT_HALO - (taps - 1)
    acc = None
    for t in range(taps):
        term = cw_ref[t:t + 1, :] * cbuf[base + t:base + t + ts, :]
        acc = term if acc is None else acc + term
    cbuf[0:SHORT_HALO, :] = cbuf[ts:ts + SHORT_HALO, :]
    y = _silu(acc)
    scale = dk ** -0.5
    for h in range(heads):
        qh = y[:, h * dk:(h + 1) * dk]
        kh = y[:, kd + h * dk:kd + (h + 1) * dk]
        q_ref[:, h * dk:(h + 1) * dk] = (
            qh * lax.rsqrt(jnp.sum(qh * qh, axis=-1, keepdims=True) + L2_EPS) * scale)
        k_ref[:, h * dk:(h + 1) * dk] = (
            kh * lax.rsqrt(jnp.sum(kh * kh, axis=-1, keepdims=True) + L2_EPS))
    v_ref[...] = y[:, 2 * kd:]


def _delta_kernel(q_ref, k_ref, v_ref, gt_ref, o_ref, state, *, heads):
    s = pl.program_id(1)
    ts = q_ref.shape[0]
    dk = q_ref.shape[1] // heads
    dv = v_ref.shape[1] // heads
    c = CHUNK

    @pl.when(s == 0)
    def _():
        state[...] = jnp.zeros(state.shape, F32)

    row = lax.broadcasted_iota(jnp.int32, (c, c), 0)
    col = lax.broadcasted_iota(jnp.int32, (c, c), 1)
    causal = row >= col
    strict = row > col
    tril = causal.astype(F32)
    eye = (row == col).astype(F32)
    sel = (lax.broadcasted_iota(jnp.int32, (heads, LANES), 0)
           == lax.broadcasted_iota(jnp.int32, (heads, LANES), 1)).astype(F32)

    for ci in range(ts // c):
        rows = slice(ci * c, (ci + 1) * c)
        gates = gt_ref[rows, :]
        gcum = jnp.dot(tril, gates, preferred_element_type=F32,
                       precision=lax.Precision.HIGHEST)
        gcum_t = lax.dot_general(sel, gcum, (((1,), (1,)), ((), ())),
                                 preferred_element_type=F32,
                                 precision=lax.Precision.HIGHEST)
        for h in range(heads):
            g_col = gcum[:, h:h + 1]
            g_row = gcum_t[h:h + 1, :]
            g_last = gcum[c - 1:c, h:h + 1]
            beta = gates[:, heads + h:heads + h + 1]
            decay = jnp.exp(jnp.where(causal, g_col - g_row, -jnp.inf))
            qh = q_ref[rows, h * dk:(h + 1) * dk]
            kh = k_ref[rows, h * dk:(h + 1) * dk]
            vh = v_ref[rows, h * dv:(h + 1) * dv]
            kb = kh * beta
            kh16 = kh.astype(BF16)
            kk = _dot_t(kb.astype(BF16), kh16) * decay
            neg = jnp.where(strict, -kk, 0.0)
            inv = eye + neg
            power = neg
            for _ in range(5):
                p16 = power.astype(BF16)
                power = _dot(p16, p16)
                inv = inv + _dot(power.astype(BF16), inv.astype(BF16))
            rhs = jnp.concatenate([vh * beta, kb * jnp.exp(g_col)], axis=-1)
            sol = _dot(inv.astype(BF16), rhs.astype(BF16))
            u = sol[:, :dv]
            w = sol[:, dv:]
            qk = _dot_t(qh.astype(BF16), kh16) * decay
            st = state[h]
            st16 = st.astype(BF16)
            v_new = u - _dot(w.astype(BF16), st16)
            v16 = v_new.astype(BF16)
            o_ref[rows, h * dv:(h + 1) * dv] = (
                _dot((qh * jnp.exp(g_col)).astype(BF16), st16) + _dot(qk.astype(BF16), v16))
            k_dec = kh * jnp.exp(g_last - g_col)
            state[h] = st * jnp.exp(g_last) + _tdot(k_dec.astype(BF16), v16)


def _gdn_out_kernel(o_ref, z_ref, h_ref, ng_ref, wo_ref, out_ref, act, *, heads):
    dv = o_ref.shape[1] // heads
    for h in range(heads):
        cols = slice(h * dv, (h + 1) * dv)
        oh = o_ref[:, cols]
        on = oh * lax.rsqrt(jnp.mean(oh * oh, axis=-1, keepdims=True) + NORM_EPS) * ng_ref[...]
        act[:, cols] = (on * _silu(z_ref[:, cols])).astype(BF16)
    out_ref[...] = h_ref[...] + _dot(act[...], wo_ref[...])


def _gdn_mixer(h, g, w_in, conv_w, a_log, dt_bias, norm_g, w_out):
    bsz, seq, d = h.shape
    heads = a_log.shape[0]
    taps, qkv_dim = conv_w.shape
    vd = w_out.shape[0]
    kd = (qkv_dim - vd) // 2
    dv = vd // heads

    w_qkvz = w_in[:, :qkv_dim + vd].astype(BF16)
    w_ab = jnp.pad(w_in[:, qkv_dim + vd:], ((0, 0), (0, LANES - 2 * heads))).astype(BF16)
    alog = jnp.pad(a_log, (0, LANES - heads)).reshape(1, LANES)
    dtb = jnp.pad(dt_bias, (0, LANES - heads)).reshape(1, LANES)

    ts = min(GDN_IN_TILE, seq)
    q, k, v, z, gates = pl.pallas_call(
        functools.partial(_gdn_in_kernel, heads=heads),
        out_shape=[jax.ShapeDtypeStruct((bsz, seq, kd), F32),
                   jax.ShapeDtypeStruct((bsz, seq, kd), F32),
                   jax.ShapeDtypeStruct((bsz, seq, vd), F32),
                   jax.ShapeDtypeStruct((bsz, seq, vd), F32),
                   jax.ShapeDtypeStruct((bsz, seq, LANES), F32)],
        grid=(bsz, seq // ts),
        in_specs=[_tile(ts, d), _full((1, d)), _full((d, qkv_dim + vd)), _full((d, LANES)),
                  _full((taps, qkv_dim)), _full((1, LANES)), _full((1, LANES))],
        out_specs=[_tile(ts, kd), _tile(ts, kd), _tile(ts, vd), _tile(ts, vd), _tile(ts, LANES)],
        scratch_shapes=[pltpu.VMEM((SHORT_HALO + ts, qkv_dim), F32)],
        compiler_params=_params(),
        name="gdn_in",
    )(h, g.reshape(1, d), w_qkvz, w_ab, conv_w, alog, dtb)

    ts = min(DELTA_TILE, seq)
    o = pl.pallas_call(
        functools.partial(_delta_kernel, heads=heads),
        out_shape=jax.ShapeDtypeStruct((bsz, seq, vd), F32),
        grid=(bsz, seq // ts),
        in_specs=[_tile(ts, kd), _tile(ts, kd), _tile(ts, vd), _tile(ts, LANES)],
        out_specs=_tile(ts, vd),
        scratch_shapes=[pltpu.VMEM((heads, kd // heads, dv), F32)],
        compiler_params=_params(),
        name="delta_rule",
    )(q, k, v, gates)

    ts = min(GDN_OUT_TILE, seq)
    return pl.pallas_call(
        functools.partial(_gdn_out_kernel, heads=heads),
        out_shape=jax.ShapeDtypeStruct(h.shape, F32),
        grid=(bsz, seq // ts),
        in_specs=[_tile(ts, vd), _tile(ts, vd), _tile(ts, d), _full((1, dv)), _full((vd, d))],
        out_specs=_tile(ts, d),
        scratch_shapes=[pltpu.VMEM((ts, vd), BF16)],
        compiler_params=_params(),
        name="gdn_out",
    )(o, z, h, norm_g.reshape(1, dv), w_out.astype(BF16))


def kernel(x, norm_mix_g, norm_ffn_g, final_norm_g, cv_w_pw1, cv_b_pw1, cv_w_dw, cv_b_dw, cv_ln_g, cv_ln_b, cv_w_pw2, cv_b_pw2, gdn_w_in, gdn_conv_w, gdn_a_log, gdn_dt_bias, gdn_norm_g, gdn_w_out, mlp_w1, mlp_w2):
    depth = norm_mix_g.shape[0]
    h = x
    for i in range(depth):
        j = i // 2
        if i % 2 == 0:
            h = _conv_mixer(h, norm_mix_g[i], cv_w_pw1[j], cv_b_pw1[j], cv_w_dw[j], cv_b_dw[j],
                            cv_ln_g[j], cv_ln_b[j], cv_w_pw2[j], cv_b_pw2[j])
        else:
            h = _gdn_mixer(h, norm_mix_g[i], gdn_w_in[j], gdn_conv_w[j], gdn_a_log[j],
                           gdn_dt_bias[j], gdn_norm_g[j], gdn_w_out[j])
        h = _mlp(h, norm_ffn_g[i], mlp_w1[i], mlp_w2[i], final_norm_g, i == depth - 1)
    return h
```

```python
import functools

import jax
import jax.numpy as jnp
from jax import lax
from jax.experimental import pallas as pl
from jax.experimental.pallas import tpu as pltpu

F32 = jnp.float32
BF16 = jnp.bfloat16
NORM_EPS = 1e-6
L2_EPS = 1e-6
CHUNK = 64
LANES = 128
SUBLANES = 8
VMEM_LIMIT_BYTES = 56 * 1024 * 1024

CONV_TILE = 512
MLP_TILE = 512
GDN_IN_TILE = 256
DELTA_TILE = 256
GDN_OUT_TILE = 512
CONV_HALO = 32
SHORT_HALO = 8
CONV_ROWS = 32
FF_CHUNK = 512


def _rms_norm(x, g):
    ms = jnp.mean(x * x, axis=-1, keepdims=True)
    return x * lax.rsqrt(ms + NORM_EPS) * g


def _silu(x):
    return x * jax.nn.sigmoid(x)


def _dot(a, b):
    return jnp.dot(a, b, preferred_element_type=F32)


def _dot_t(a, b):
    return lax.dot_general(a, b, (((1,), (1,)), ((), ())), preferred_element_type=F32)


def _tdot(a, b):
    return lax.dot_general(a, b, (((0,), (0,)), ((), ())), preferred_element_type=F32)


def _full(shape):
    return pl.BlockSpec(shape, lambda b, s: (0,) * len(shape))


def _tile(ts, d):
    return pl.BlockSpec((None, ts, d), lambda b, s: (b, s, 0))


def _params():
    return pltpu.CompilerParams(
        dimension_semantics=("arbitrary", "arbitrary"),
        vmem_limit_bytes=VMEM_LIMIT_BYTES)


def _conv_mixer_kernel(x_ref, g_ref, w1_ref, b1_ref, wdw_ref, bdw_ref, lng_ref, lnb_ref,
                       w2_ref, b2_ref, o_ref, ubuf, cvo, act):
    s = pl.program_id(1)
    ts, d = x_ref.shape
    slabs, taps = wdw_ref.shape[0], wdw_ref.shape[1]
    hn = _rms_norm(x_ref[...], g_ref[...]).astype(BF16)
    u = _dot(hn, w1_ref[...]) + b1_ref[...]
    glu = u[:, :d] * jax.nn.sigmoid(u[:, d:])

    @pl.when(s == 0)
    def _():
        ubuf[:, 0:CONV_HALO, :] = jnp.zeros((slabs, CONV_HALO, LANES), F32)

    for lt in range(slabs):
        ubuf[lt, CONV_HALO:CONV_HALO + ts, :] = glu[:, lt * LANES:(lt + 1) * LANES]
    base = CONV_HALO - (taps - 1)
    stride = CONV_ROWS // SUBLANES

    for lt in range(slabs):
        w = [wdw_ref[lt, k] for k in range(taps)]
        bias = bdw_ref[lt]

        def rows(c, carry, lt=lt, w=w, bias=bias):
            r0 = c * CONV_ROWS
            acc = [bias] * stride
            for m in range(taps + stride - 1):
                win = ubuf[lt, pl.ds(r0 + base + m, SUBLANES, stride=stride), :]
                for j in range(stride):
                    if 0 <= m - j < taps:
                        acc[j] = acc[j] + w[m - j] * win
            for j in range(stride):
                cvo[lt, pl.ds(r0 + j, SUBLANES, stride=stride), :] = acc[j]
            return carry

        lax.fori_loop(0, ts // CONV_ROWS, rows, 0)

    for lt in range(slabs):
        ubuf[lt, 0:CONV_HALO, :] = ubuf[lt, ts:ts + CONV_HALO, :]

    def norm_rows(c, carry):
        r0 = pl.multiple_of(c * CONV_ROWS, CONV_ROWS)
        xs = [cvo[lt, pl.ds(r0, CONV_ROWS), :] for lt in range(slabs)]
        tot = xs[0]
        for lt in range(1, slabs):
            tot = tot + xs[lt]
        mu = jnp.sum(tot, axis=-1, keepdims=True) * (1.0 / d)
        xc = [x - mu for x in xs]
        sq = xc[0] * xc[0]
        for lt in range(1, slabs):
            sq = sq + xc[lt] * xc[lt]
        inv = lax.rsqrt(jnp.sum(sq, axis=-1, keepdims=True) * (1.0 / d) + NORM_EPS)
        for lt in range(slabs):
            cols = slice(lt * LANES, (lt + 1) * LANES)
            y = xc[lt] * inv * lng_ref[:, cols] + lnb_ref[:, cols]
            act[pl.ds(r0, CONV_ROWS), cols] = _silu(y).astype(BF16)
        return carry

    lax.fori_loop(0, ts // CONV_ROWS, norm_rows, 0)
    o_ref[...] = x_ref[...] + _dot(act[...], w2_ref[...]) + b2_ref[...]


def _conv_mixer(h, g, w1, b1, wdw, bdw, lng, lnb, w2, b2):
    bsz, seq, d = h.shape
    ts = min(CONV_TILE, seq)
    taps = wdw.shape[0]
    slabs = d // LANES
    wdw_b = jnp.broadcast_to(wdw.reshape(taps, slabs, 1, LANES).transpose(1, 0, 2, 3),
                             (slabs, taps, SUBLANES, LANES))
    bdw_b = jnp.broadcast_to(bdw.reshape(slabs, 1, LANES), (slabs, SUBLANES, LANES))
    return pl.pallas_call(
        _conv_mixer_kernel,
        out_shape=jax.ShapeDtypeStruct(h.shape, F32),
        grid=(bsz, seq // ts),
        in_specs=[_tile(ts, d), _full((1, d)), _full((d, 2 * d)), _full((1, 2 * d)),
                  _full((slabs, taps, SUBLANES, LANES)), _full((slabs, SUBLANES, LANES)),
                  _full((1, d)), _full((1, d)), _full((d, d)), _full((1, d))],
        out_specs=_tile(ts, d),
        scratch_shapes=[pltpu.VMEM((slabs, CONV_HALO + ts, LANES), F32),
                        pltpu.VMEM((slabs, ts, LANES), F32),
                        pltpu.VMEM((ts, d), BF16)],
        compiler_params=_params(),
        name="conv_mixer",
    )(h, g.reshape(1, d), w1.astype(BF16), b1.reshape(1, 2 * d), wdw_b, bdw_b,
      lng.reshape(1, d), lnb.reshape(1, d), w2.astype(BF16), b2.reshape(1, d))


def _mlp_kernel(h_ref, g_ref, w1_ref, w2_ref, fg_ref, o_ref, *, final_norm):
    d_ff = w1_ref.shape[1]
    hn = _rms_norm(h_ref[...], g_ref[...]).astype(BF16)
    acc = None
    for j in range(d_ff // FF_CHUNK):
        cols = slice(j * FF_CHUNK, (j + 1) * FF_CHUNK)
        a = jnp.maximum(_dot(hn, w1_ref[:, cols]), 0.0)
        p = _dot((a * a).astype(BF16), w2_ref[cols, :])
        acc = p if acc is None else acc + p
    y = h_ref[...] + acc
    if final_norm:
        y = _rms_norm(y, fg_ref[...])
    o_ref[...] = y


def _mlp(h, g, w1, w2, final_g, final_norm):
    bsz, seq, d = h.shape
    d_ff = w1.shape[1]
    ts = min(MLP_TILE, seq)
    return pl.pallas_call(
        functools.partial(_mlp_kernel, final_norm=final_norm),
        out_shape=jax.ShapeDtypeStruct(h.shape, F32),
        grid=(bsz, seq // ts),
        in_specs=[_tile(ts, d), _full((1, d)), _full((d, d_ff)), _full((d_ff, d)), _full((1, d))],
        out_specs=_tile(ts, d),
        compiler_params=_params(),
        name="mlp_final" if final_norm else "mlp",
    )(h, g.reshape(1, d), w1.astype(BF16), w2.astype(BF16), final_g.reshape(1, d))


def _gdn_in_kernel(h_ref, g_ref, wq_ref, wab_ref, cw_ref, alog_ref, dtb_ref,
                   q_ref, k_ref, v_ref, z_ref, gt_ref, cbuf, *, heads):
    s = pl.program_id(1)
    ts, d = h_ref.shape
    kd = q_ref.shape[1]
    dk = kd // heads
    qkv_dim = cbuf.shape[1]
    taps = cw_ref.shape[0]
    hn = _rms_norm(h_ref[...], g_ref[...]).astype(BF16)
    proj = _dot(hn, wq_ref[...])
    z_ref[...] = proj[:, qkv_dim:]

    ab = _dot(hn, wab_ref[...])
    pre = ab + dtb_ref[...]
    softplus = jnp.maximum(pre, 0.0) + jnp.log1p(jnp.exp(-jnp.abs(pre)))
    lane = lax.broadcasted_iota(jnp.int32, ab.shape, 1)
    gt_ref[...] = jnp.where(lane < heads, -jnp.exp(alog_ref[...]) * softplus, jax.nn.sigmoid(ab))

    @pl.when(s == 0)
    def _():
        cbuf[0:SHORT_HALO, :] = jnp.zeros((SHORT_HALO, qkv_dim), F32)

    cbuf[SHORT_HALO:SHORT_HALO + ts, :] = proj[:, :qkv_dim]
    base = SHORT_HALO - (taps - 1)
    acc = None
    for t in range(taps):
        term = cw_ref[t:t + 1, :] * cbuf[base + t:base + t + ts, :]
        acc = term if acc is None else acc + term
    cbuf[0:SHORT_HALO, :] = cbuf[ts:ts + SHORT_HALO, :]
    y = _silu(acc)
    scale = dk ** -0.5
    for h in range(heads):
        qh = y[:, h * dk:(h + 1) * dk]
        kh = y[:, kd + h * dk:kd + (h + 1) * dk]
        q_ref[:, h * dk:(h + 1) * dk] = (
            qh * lax.rsqrt(jnp.sum(qh * qh, axis=-1, keepdims=True) + L2_EPS) * scale)
        k_ref[:, h * dk:(h + 1) * dk] = (
            kh * lax.rsqrt(jnp.sum(kh * kh, axis=-1, keepdims=True) + L2_EPS))
    v_ref[...] = y[:, 2 * kd:]


def _delta_kernel(q_ref, k_ref, v_ref, gt_ref, o_ref, state, *, heads):
    s = pl.program_id(1)
    ts = q_ref.shape[0]
    dk = q_ref.shape[1] // heads
    dv = v_ref.shape[1] // heads
    c = CHUNK

    @pl.when(s == 0)
    def _():
        state[...] = jnp.zeros(state.shape, F32)

    row = lax.broadcasted_iota(jnp.int32, (c, 2 * c), 0)
    lane = lax.broadcasted_iota(jnp.int32, (c, 2 * c), 1)
    col = jnp.where(lane < c, lane, lane - c)
    low = lane < c
    causal = row >= col
    strict = row > col
    eye_high = (lane == row + c).astype(F32)
    tril = (lax.broadcasted_iota(jnp.int32, (c, c), 0)
            >= lax.broadcasted_iota(jnp.int32, (c, c), 1)).astype(F32)
    sel = (lax.broadcasted_iota(jnp.int32, (heads, LANES), 0)
           == lax.broadcasted_iota(jnp.int32, (heads, LANES), 1)).astype(F32)
    n_rounds = (c - 1).bit_length()
    prepared = {}

    def intra(ci):
        rows = slice(ci * c, (ci + 1) * c)
        gates = gt_ref[rows, :]
        gcum = jnp.dot(tril, gates, preferred_element_type=F32,
                       precision=lax.Precision.HIGHEST)
        gcum_t = lax.dot_general(sel, jnp.concatenate([gcum, gcum], axis=0),
                                 (((1,), (1,)), ((), ())), preferred_element_type=F32,
                                 precision=lax.Precision.HIGHEST)
        yield
        x1, rhs16, qg16, kdec16, decay, g_last = [], [], [], [], [], []
        for h in range(heads):
            g_col = gcum[:, h:h + 1]
            g_last.append(gcum[c - 1:c, h:h + 1])
            beta = gates[:, heads + h:heads + h + 1]
            decay.append(jnp.exp(jnp.where(causal, g_col - gcum_t[h:h + 1, :], -jnp.inf)))
            qh = q_ref[rows, h * dk:(h + 1) * dk]
            kh = k_ref[rows, h * dk:(h + 1) * dk]
            vh = v_ref[rows, h * dv:(h + 1) * dv]
            kb = kh * beta
            k16 = kh.astype(BF16)
            e_col = jnp.exp(g_col)
            x1.append(_dot_t(jnp.concatenate([kb.astype(BF16), qh.astype(BF16)], axis=0),
                             jnp.concatenate([k16, k16], axis=0)))
            rhs16.append(jnp.concatenate([vh * beta, kb * e_col], axis=-1).astype(BF16))
            qg16.append((qh * e_col).astype(BF16))
            kdec16.append((kh * jnp.exp(g_last[h] - g_col)).astype(BF16))
        yield
        run, qk16 = [], []
        for h in range(heads):
            neg = jnp.where(strict, -(x1[h][:c] * decay[h]), 0.0)
            qk16.append((x1[h][c:, :c] * decay[h][:, :c]).astype(BF16))
            run.append(jnp.where(low, neg, eye_high))
        zeros16 = jnp.zeros((c, 2 * c), BF16)
        for _ in range(n_rounds):
            prod = []
            for h in range(heads):
                hi16 = run[h].astype(BF16)
                lo = run[h] - hi16.astype(F32)
                lhs16 = jnp.concatenate(
                    [jnp.where(low, run[h], pltpu.roll(lo, c, axis=1)).astype(BF16), hi16], axis=1)
                rhs_stack = jnp.concatenate([hi16, hi16, lo.astype(BF16), zeros16], axis=0)
                prod.append(_dot(lhs16, rhs_stack))
            yield
            run = [jnp.where(low, prod[h], run[h] + prod[h]) for h in range(heads)]
        sol = []
        for h in range(heads):
            inv16 = run[h].astype(BF16)
            sol.append(_dot(inv16, jnp.concatenate([jnp.zeros_like(rhs16[h]), rhs16[h]], axis=0)))
        yield
        prepared[ci] = dict(
            u=[x[:, :dv] for x in sol], w16=[x[:, dv:].astype(BF16) for x in sol],
            qg16=qg16, qk16=qk16, kdec16=kdec16, g_last=g_last)

    def scan(ci):
        rows = slice(ci * c, (ci + 1) * c)
        p = prepared.pop(ci)
        proj = []
        for h in range(heads):
            st16 = state[h].astype(BF16)
            proj.append(_dot(jnp.concatenate([p["w16"][h], p["qg16"][h]], axis=0), st16))
        yield
        o_intra, d_state = [], []
        for h in range(heads):
            v16 = (p["u"][h] - proj[h][:c]).astype(BF16)
            o_intra.append(_dot(p["qk16"][h], v16))
            d_state.append(_tdot(p["kdec16"][h], v16))
        yield
        for h in range(heads):
            o_ref[rows, h * dv:(h + 1) * dv] = proj[h][c:] + o_intra[h]
            state[h] = state[h] * jnp.exp(p["g_last"][h]) + d_state[h]

    def run_interleaved(*gens):
        live = list(gens)
        while live:
            for g in list(live):
                try:
                    next(g)
                except StopIteration:
                    live.remove(g)

    run_interleaved(intra(0))
    n_chunks = ts // c
    for ci in range(n_chunks):
        if ci + 1 < n_chunks:
            run_interleaved(scan(ci), intra(ci + 1))
        else:
            run_interleaved(scan(ci))


def _gdn_out_kernel(o_ref, z_ref, h_ref, ng_ref, wo_ref, out_ref, act, *, heads):
    dv = o_ref.shape[1] // heads
    for h in range(heads):
        cols = slice(h * dv, (h + 1) * dv)
        oh = o_ref[:, cols]
        on = oh * lax.rsqrt(jnp.mean(oh * oh, axis=-1, keepdims=True) + NORM_EPS) * ng_ref[...]
        act[:, cols] = (on * _silu(z_ref[:, cols])).astype(BF16)
    out_ref[...] = h_ref[...] + _dot(act[...], wo_ref[...])


def _gdn_mixer(h, g, w_in, conv_w, a_log, dt_bias, norm_g, w_out):
    bsz, seq, d = h.shape
    heads = a_log.shape[0]
    taps, qkv_dim = conv_w.shape
    vd = w_out.shape[0]
    kd = (qkv_dim - vd) // 2
    dv = vd // heads

    w_qkvz = w_in[:, :qkv_dim + vd].astype(BF16)
    w_ab = jnp.pad(w_in[:, qkv_dim + vd:], ((0, 0), (0, LANES - 2 * heads))).astype(BF16)
    alog = jnp.pad(a_log, (0, LANES - heads)).reshape(1, LANES)
    dtb = jnp.pad(dt_bias, (0, LANES - heads)).reshape(1, LANES)

    ts = min(GDN_IN_TILE, seq)
    q, k, v, z, gates = pl.pallas_call(
        functools.partial(_gdn_in_kernel, heads=heads),
        out_shape=[jax.ShapeDtypeStruct((bsz, seq, kd), F32),
                   jax.ShapeDtypeStruct((bsz, seq, kd), F32),
                   jax.ShapeDtypeStruct((bsz, seq, vd), F32),
                   jax.ShapeDtypeStruct((bsz, seq, vd), F32),
                   jax.ShapeDtypeStruct((bsz, seq, LANES), F32)],
        grid=(bsz, seq // ts),
        in_specs=[_tile(ts, d), _full((1, d)), _full((d, qkv_dim + vd)), _full((d, LANES)),
                  _full((taps, qkv_dim)), _full((1, LANES)), _full((1, LANES))],
        out_specs=[_tile(ts, kd), _tile(ts, kd), _tile(ts, vd), _tile(ts, vd), _tile(ts, LANES)],
        scratch_shapes=[pltpu.VMEM((SHORT_HALO + ts, qkv_dim), F32)],
        compiler_params=_params(),
        name="gdn_in",
    )(h, g.reshape(1, d), w_qkvz, w_ab, conv_w, alog, dtb)

    ts = min(DELTA_TILE, seq)
    o = pl.pallas_call(
        functools.partial(_delta_kernel, heads=heads),
        out_shape=jax.ShapeDtypeStruct((bsz, seq, vd), F32),
        grid=(bsz, seq // ts),
        in_specs=[_tile(ts, kd), _tile(ts, kd), _tile(ts, vd), _tile(ts, LANES)],
        out_specs=_tile(ts, vd),
        scratch_shapes=[pltpu.VMEM((heads, kd // heads, dv), F32)],
        compiler_params=_params(),
        name="delta_rule",
    )(q, k, v, gates)

    ts = min(GDN_OUT_TILE, seq)
    return pl.pallas_call(
        functools.partial(_gdn_out_kernel, heads=heads),
        out_shape=jax.ShapeDtypeStruct(h.shape, F32),
        grid=(bsz, seq // ts),
        in_specs=[_tile(ts, vd), _tile(ts, vd), _tile(ts, d), _full((1, dv)), _full((vd, d))],
        out_specs=_tile(ts, d),
        scratch_shapes=[pltpu.VMEM((ts, vd), BF16)],
        compiler_params=_params(),
        name="gdn_out",
    )(o, z, h, norm_g.reshape(1, dv), w_out.astype(BF16))


def kernel(x, norm_mix_g, norm_ffn_g, final_norm_g, cv_w_pw1, cv_b_pw1, cv_w_dw, cv_b_dw, cv_ln_g, cv_ln_b, cv_w_pw2, cv_b_pw2, gdn_w_in, gdn_conv_w, gdn_a_log, gdn_dt_bias, gdn_norm_g, gdn_w_out, mlp_w1, mlp_w2):
    depth = norm_mix_g.shape[0]
    h = x
    for i in range(depth):
        j = i // 2
        if i % 2 == 0:
            h = _conv_mixer(h, norm_mix_g[i], cv_w_pw1[j], cv_b_pw1[j], cv_w_dw[j], cv_b_dw[j],
                            cv_ln_g[j], cv_ln_b[j], cv_w_pw2[j], cv_b_pw2[j])
        else:
            h = _gdn_mixer(h, norm_mix_g[i], gdn_w_in[j], gdn_conv_w[j], gdn_a_log[j],
                           gdn_dt_bias[j], gdn_norm_g[j], gdn_w_out[j])
        h = _mlp(h, norm_ffn_g[i], mlp_w1[i], mlp_w2[i], final_norm_g, i == depth - 1)
    return h
```

```python
import functools

import jax
import jax.numpy as jnp
from jax import lax
from jax.experimental import pallas as pl
from jax.experimental.pallas import tpu as pltpu

F32 = jnp.float32
BF16 = jnp.bfloat16
NORM_EPS = 1e-6
L2_EPS = 1e-6
CHUNK = 64
INV_BLOCK = 16
LANES = 128
SUBLANES = 8
VMEM_LIMIT_BYTES = 56 * 1024 * 1024

CONV_TILE = 512
MLP_TILE = 512
GDN_IN_TILE = 256
DELTA_TILE = 256
DELTA_GROUP = 2
CONV_HALO = 32
SHORT_HALO = 8
CONV_ROWS = 32
FF_CHUNK = 512


def _rms_norm(x, g):
    ms = jnp.mean(x * x, axis=-1, keepdims=True)
    return x * lax.rsqrt(ms + NORM_EPS) * g


def _silu(x):
    return x * jax.nn.sigmoid(x)


def _dot(a, b):
    return jnp.dot(a, b, preferred_element_type=F32)


def _dot_t(a, b):
    return lax.dot_general(a, b, (((1,), (1,)), ((), ())), preferred_element_type=F32)


def _tdot(a, b):
    return lax.dot_general(a, b, (((0,), (0,)), ((), ())), preferred_element_type=F32)


def _full(shape):
    return pl.BlockSpec(shape, lambda b, s: (0,) * len(shape))


def _tile(ts, d):
    return pl.BlockSpec((None, ts, d), lambda b, s: (b, s, 0))


def _params():
    return pltpu.CompilerParams(
        dimension_semantics=("arbitrary", "arbitrary"),
        vmem_limit_bytes=VMEM_LIMIT_BYTES)


def _conv_mixer_kernel(x_ref, g_ref, w1_ref, b1_ref, wdw_ref, bdw_ref, lng_ref, lnb_ref,
                       w2_ref, b2_ref, o_ref, *scratch):
    s = pl.program_id(1)
    ts, d = x_ref.shape
    slabs, taps = wdw_ref.shape[0], wdw_ref.shape[1]
    ubufs, cvos, act = scratch[:slabs], scratch[slabs:2 * slabs], scratch[2 * slabs]
    base = CONV_HALO - (taps - 1)
    stride = CONV_ROWS // SUBLANES

    @pl.when(s == 0)
    def _():
        for lt in range(slabs):
            ubufs[lt][0:CONV_HALO, :] = jnp.zeros((CONV_HALO, LANES), F32)

    hn = _rms_norm(x_ref[...], g_ref[...]).astype(BF16)
    for lt in range(slabs):
        cols = slice(2 * lt * LANES, 2 * (lt + 1) * LANES)
        u = _dot(hn, w1_ref[:, cols]) + b1_ref[:, cols]
        ubuf, cvo = ubufs[lt], cvos[lt]
        ubuf[CONV_HALO:CONV_HALO + ts, :] = u[:, :LANES] * jax.nn.sigmoid(u[:, LANES:])
        w = [wdw_ref[lt, k] for k in range(taps)]
        bias = bdw_ref[lt]
        for r0 in range(0, ts, CONV_ROWS):
            acc = [bias] * stride
            for m in range(taps + stride - 1):
                win = ubuf[pl.ds(r0 + base + m, SUBLANES, stride=stride), :]
                for j in range(stride):
                    if 0 <= m - j < taps:
                        acc[j] = acc[j] + w[m - j] * win
            for j in range(stride):
                cvo[pl.ds(r0 + j, SUBLANES, stride=stride), :] = acc[j]
        ubuf[0:CONV_HALO, :] = ubuf[ts:ts + CONV_HALO, :]

    def norm_rows(c, carry):
        for half in range(2):
            r0 = pl.multiple_of((2 * c + half) * CONV_ROWS, CONV_ROWS)
            xs = [cvos[lt][pl.ds(r0, CONV_ROWS), :] for lt in range(slabs)]
            tot = xs[0]
            for lt in range(1, slabs):
                tot = tot + xs[lt]
            mu = jnp.sum(tot, axis=-1, keepdims=True) * (1.0 / d)
            xc = [x - mu for x in xs]
            sq = xc[0] * xc[0]
            for lt in range(1, slabs):
                sq = sq + xc[lt] * xc[lt]
            inv = lax.rsqrt(jnp.sum(sq, axis=-1, keepdims=True) * (1.0 / d) + NORM_EPS)
            for lt in range(slabs):
                cols = slice(lt * LANES, (lt + 1) * LANES)
                y = xc[lt] * inv * lng_ref[:, cols] + lnb_ref[:, cols]
                act[pl.ds(r0, CONV_ROWS), cols] = _silu(y).astype(BF16)
        return carry

    lax.fori_loop(0, ts // (2 * CONV_ROWS), norm_rows, 0)
    o_ref[...] = x_ref[...] + _dot(act[...], w2_ref[...]) + b2_ref[...]


def _conv_mixer(h, g, w1, b1, wdw, bdw, lng, lnb, w2, b2):
    bsz, seq, d = h.shape
    ts = min(CONV_TILE, seq)
    taps = wdw.shape[0]
    slabs = d // LANES
    wdw_b = jnp.broadcast_to(wdw.reshape(taps, slabs, 1, LANES).transpose(1, 0, 2, 3),
                             (slabs, taps, SUBLANES, LANES))
    bdw_b = jnp.broadcast_to(bdw.reshape(slabs, 1, LANES), (slabs, SUBLANES, LANES))
    w1_g = w1.reshape(d, 2, slabs, LANES).transpose(0, 2, 1, 3).reshape(d, 2 * d).astype(BF16)
    b1_g = b1.reshape(2, slabs, LANES).transpose(1, 0, 2).reshape(1, 2 * d)
    return pl.pallas_call(
        _conv_mixer_kernel,
        out_shape=jax.ShapeDtypeStruct(h.shape, F32),
        grid=(bsz, seq // ts),
        in_specs=[_tile(ts, d), _full((1, d)), _full((d, 2 * d)), _full((1, 2 * d)),
                  _full((slabs, taps, SUBLANES, LANES)), _full((slabs, SUBLANES, LANES)),
                  _full((1, d)), _full((1, d)), _full((d, d)), _full((1, d))],
        out_specs=_tile(ts, d),
        scratch_shapes=([pltpu.VMEM((CONV_HALO + ts, LANES), F32)] * slabs
                        + [pltpu.VMEM((ts, LANES), F32)] * slabs
                        + [pltpu.VMEM((ts, d), BF16)]),
        compiler_params=_params(),
        name="conv_mixer",
    )(h, g.reshape(1, d), w1_g, b1_g, wdw_b, bdw_b,
      lng.reshape(1, d), lnb.reshape(1, d), w2.astype(BF16), b2.reshape(1, d))


def _mlp_kernel(h_ref, g_ref, w1_ref, w2_ref, fg_ref, o_ref, *, final_norm):
    d_ff = w1_ref.shape[1]
    hn = _rms_norm(h_ref[...], g_ref[...]).astype(BF16)
    acc = None
    for j in range(d_ff // FF_CHUNK):
        cols = slice(j * FF_CHUNK, (j + 1) * FF_CHUNK)
        a = jnp.maximum(_dot(hn, w1_ref[:, cols]), 0.0)
        p = _dot((a * a).astype(BF16), w2_ref[cols, :])
        acc = p if acc is None else acc + p
    y = h_ref[...] + acc
    if final_norm:
        y = _rms_norm(y, fg_ref[...])
    o_ref[...] = y


def _mlp(h, g, w1, w2, final_g, final_norm):
    bsz, seq, d = h.shape
    d_ff = w1.shape[1]
    ts = min(MLP_TILE, seq)
    return pl.pallas_call(
        functools.partial(_mlp_kernel, final_norm=final_norm),
        out_shape=jax.ShapeDtypeStruct(h.shape, F32),
        grid=(bsz, seq // ts),
        in_specs=[_tile(ts, d), _full((1, d)), _full((d, d_ff)), _full((d_ff, d)), _full((1, d))],
        out_specs=_tile(ts, d),
        compiler_params=_params(),
        name="mlp_final" if final_norm else "mlp",
    )(h, g.reshape(1, d), w1.astype(BF16), w2.astype(BF16), final_g.reshape(1, d))


def _gdn_in_kernel(h_ref, g_ref, wq_ref, wab_ref, cw_ref, alog_ref, dtb_ref,
                   q_ref, k_ref, v_ref, z_ref, gt_ref, *cbufs, heads):
    s = pl.program_id(1)
    ts, d = h_ref.shape
    slabs, taps = cw_ref.shape[0], cw_ref.shape[1]
    qkv_dim = slabs * LANES
    base = SHORT_HALO - (taps - 1)
    stride = CONV_ROWS // SUBLANES
    q_scale = LANES ** -0.5

    @pl.when(s == 0)
    def _():
        for sl in range(slabs):
            cbufs[sl][0:SHORT_HALO, :] = jnp.zeros((SHORT_HALO, LANES), F32)

    hn = _rms_norm(h_ref[...], g_ref[...]).astype(BF16)

    ab = _dot(hn, wab_ref[...])
    pre = ab + dtb_ref[...]
    softplus = jnp.maximum(pre, 0.0) + jnp.log1p(jnp.exp(-jnp.abs(pre)))
    lane = lax.broadcasted_iota(jnp.int32, ab.shape, 1)
    gt_ref[...] = jnp.where(lane < heads, -jnp.exp(alog_ref[...]) * softplus, jax.nn.sigmoid(ab))

    for blk in range(slabs // 2):
        proj = _dot(hn, wq_ref[:, 2 * blk * LANES:2 * (blk + 1) * LANES])
        for sl in (2 * blk, 2 * blk + 1):
            cbuf = cbufs[sl]
            cbuf[SHORT_HALO:SHORT_HALO + ts, :] = proj[:, (sl % 2) * LANES:(sl % 2 + 1) * LANES]
            w = [cw_ref[sl, t] for t in range(taps)]
            for r0 in range(0, ts, CONV_ROWS):
                win = [cbuf[pl.ds(r0 + base + m, SUBLANES, stride=stride), :]
                       for m in range(taps + stride - 1)]
                for j in range(stride):
                    acc = w[0] * win[j]
                    for t in range(1, taps):
                        acc = acc + w[t] * win[j + t]
                    y = _silu(acc)
                    dst = pl.ds(r0 + j, SUBLANES, stride=stride)
                    if sl < heads:
                        q_ref[sl, dst, :] = (
                            y * lax.rsqrt(jnp.sum(y * y, axis=-1, keepdims=True) + L2_EPS) * q_scale)
                    elif sl < 2 * heads:
                        k_ref[sl - heads, dst, :] = (
                            y * lax.rsqrt(jnp.sum(y * y, axis=-1, keepdims=True) + L2_EPS))
                    else:
                        v_ref[sl - 2 * heads, dst, :] = y
            cbuf[0:SHORT_HALO, :] = cbuf[ts:ts + SHORT_HALO, :]
    z_ref[...] = _dot(hn, wq_ref[:, qkv_dim:])


def _delta_kernel(q_ref, k_ref, v_ref, gt_ref, z_ref, h_ref, ng_ref, wo_ref, out_ref,
                  state, o_buf, act, *, heads):
    s = pl.program_id(1)
    ts = q_ref.shape[1]
    dv = v_ref.shape[2]
    c = CHUNK
    n_chunks = ts // c

    @pl.when(s == 0)
    def _():
        state[...] = jnp.zeros(state.shape, F32)

    row = lax.broadcasted_iota(jnp.int32, (c, 2 * c), 0)
    lane = lax.broadcasted_iota(jnp.int32, (c, 2 * c), 1)
    col = jnp.where(lane < c, lane, lane - c)
    low = lane < c
    causal = row >= col
    strict = row > col
    eye_low = (lane == row).astype(F32)
    diag_block = strict & (row // INV_BLOCK == col // INV_BLOCK)
    merge_sizes = []
    b = INV_BLOCK
    while b < c:
        merge_sizes.append(b)
        b *= 2
    merge_masks = [low & (row // (2 * b) == col // (2 * b)) & (row // b == col // b + 1)
                   for b in merge_sizes]
    n_rounds = (INV_BLOCK - 1).bit_length()
    tril = (lax.broadcasted_iota(jnp.int32, (c, c), 0)
            >= lax.broadcasted_iota(jnp.int32, (c, c), 1)).astype(F32)
    sel = (lax.broadcasted_iota(jnp.int32, (heads, LANES), 0)
           == lax.broadcasted_iota(jnp.int32, (heads, LANES), 1)).astype(F32)
    prepared = {}

    def side_by_side(a, b):
        return jnp.concatenate([a, b], axis=1)

    def block_diag(a, b, pad_rows=0):
        za, zb = jnp.zeros_like(a), jnp.zeros_like(b)
        blocks = [jnp.concatenate([a, zb], axis=1)]
        if pad_rows:
            blocks.append(jnp.zeros((pad_rows, a.shape[1] + b.shape[1]), a.dtype))
        blocks.append(jnp.concatenate([za, b], axis=1))
        if pad_rows:
            blocks.append(jnp.zeros((pad_rows, a.shape[1] + b.shape[1]), a.dtype))
        return jnp.concatenate(blocks, axis=0)

    def block_diag_high(a, b):
        z = jnp.zeros((c, a.shape[1] + b.shape[1]), a.dtype)
        return jnp.concatenate([z, jnp.concatenate([a, jnp.zeros_like(b)], axis=1),
                                z, jnp.concatenate([jnp.zeros_like(a), b], axis=1)], axis=0)

    def pair_dot(lhs_a, lhs_b, rhs):
        out = _dot(side_by_side(lhs_a, lhs_b), rhs)
        half = out.shape[1] // 2
        return out[:, :half], out[:, half:]

    pairs = [(h, h + 1) for h in range(0, heads, 2)]

    def intra(chunks):
        gates, gcum, gcum_t = {}, {}, {}
        for ci in chunks:
            gates[ci] = gt_ref[ci * c:(ci + 1) * c, :]
            gcum[ci] = jnp.dot(tril, gates[ci], preferred_element_type=F32,
                               precision=lax.Precision.HIGHEST)
            gcum_t[ci] = lax.dot_general(
                sel, jnp.concatenate([gcum[ci], gcum[ci]], axis=0), (((1,), (1,)), ((), ())),
                preferred_element_type=F32, precision=lax.Precision.HIGHEST)
        yield
        units = [(ci, h) for ci in chunks for h in range(heads)]
        duos = [((ci, ha), (ci, hb)) for ci in chunks for ha, hb in pairs]
        lhs1, kt16, rhs16, qg16, kdec_t16, decay, g_last = {}, {}, {}, {}, {}, {}, {}
        for u in units:
            ci, h = u
            rows = slice(ci * c, (ci + 1) * c)
            g_col = gcum[ci][:, h:h + 1]
            g_row = gcum_t[ci][h:h + 1, :]
            g_last[u] = gcum[ci][c - 1:c, h:h + 1]
            beta = gates[ci][:, heads + h:heads + h + 1]
            decay[u] = jnp.exp(jnp.where(causal, g_col - g_row, -jnp.inf))
            qh = q_ref[h, rows, :]
            kh = k_ref[h, rows, :]
            vh = v_ref[h, rows, :]
            kb = kh * beta
            e_col = jnp.exp(g_col)
            k_t = jnp.concatenate([kh, kh], axis=0).T
            kt16[u] = k_t.astype(BF16)
            kdec_t16[u] = (k_t * jnp.exp(g_last[u] - g_row)).astype(BF16)
            lhs1[u] = jnp.concatenate([kb.astype(BF16), qh.astype(BF16)], axis=0)
            rhs16[u] = jnp.concatenate([vh * beta, kb * e_col], axis=-1).astype(BF16)
            qg16[u] = (qh * e_col).astype(BF16)
        x1 = {}
        for ua, ub in duos:
            x1[ua], x1[ub] = pair_dot(lhs1[ua], lhs1[ub], block_diag(kt16[ua], kt16[ub]))
        yield
        run, lower16, qk16 = {}, {}, {}
        for u in units:
            lower = x1[u][:c] * decay[u]
            lower16[u] = lower.astype(BF16)
            qk16[u] = (x1[u][c:] * decay[u]).astype(BF16)
            run[u] = jnp.where(low, eye_low, jnp.where(diag_block, -lower, 0.0))
        for _ in range(n_rounds):
            prod = {}
            for ua, ub in duos:
                a16, b16 = run[ua].astype(BF16), run[ub].astype(BF16)
                prod[ua], prod[ub] = pair_dot(a16, b16, block_diag_high(a16, b16))
            yield
            run = {u: jnp.where(low, run[u] + prod[u], prod[u]) for u in units}
        for mask in merge_masks:
            tmp = {}
            for ua, ub in duos:
                ca16 = jnp.where(mask, lower16[ua], jnp.zeros_like(lower16[ua]))
                cb16 = jnp.where(mask, lower16[ub], jnp.zeros_like(lower16[ub]))
                tmp[ua], tmp[ub] = pair_dot(
                    ca16, cb16, block_diag(run[ua].astype(BF16), run[ub].astype(BF16), pad_rows=c))
            yield
            upd = {}
            for ua, ub in duos:
                upd[ua], upd[ub] = pair_dot(
                    run[ua].astype(BF16), run[ub].astype(BF16),
                    block_diag(tmp[ua].astype(BF16), tmp[ub].astype(BF16), pad_rows=c))
            yield
            run = {u: run[u] - upd[u] for u in units}
        sol = {}
        for ua, ub in duos:
            sol[ua], sol[ub] = pair_dot(run[ua].astype(BF16), run[ub].astype(BF16),
                                        block_diag(rhs16[ua], rhs16[ub], pad_rows=c))
        yield
        for u in units:
            prepared[u] = dict(
                u=sol[u][:, :dv], qg16=qg16[u], g_last=g_last[u],
                lhs_proj=jnp.concatenate([sol[u][:, dv:].astype(BF16), qg16[u]], axis=0),
                lhs_intra=jnp.concatenate([qk16[u], kdec_t16[u]], axis=0))

    def scan(ci):
        rows = slice(ci * c, (ci + 1) * c)
        p = [prepared.pop((ci, h)) for h in range(heads)]
        proj = [None] * heads
        for ha, hb in pairs:
            proj[ha], proj[hb] = pair_dot(
                p[ha]["lhs_proj"], p[hb]["lhs_proj"],
                block_diag(state[ha].astype(BF16), state[hb].astype(BF16)))
        yield
        intra_out = [None] * heads
        for ha, hb in pairs:
            va16 = (p[ha]["u"] - proj[ha][:c]).astype(BF16)
            vb16 = (p[hb]["u"] - proj[hb][:c]).astype(BF16)
            intra_out[ha], intra_out[hb] = pair_dot(
                p[ha]["lhs_intra"], p[hb]["lhs_intra"], block_diag(va16, vb16, pad_rows=c))
        yield
        for h in range(heads):
            o_buf[rows, h * dv:(h + 1) * dv] = proj[h][c:] + intra_out[h][:c]
            state[h] = state[h] * jnp.exp(p[h]["g_last"]) + intra_out[h][c:]
        yield

    def scans(chunks):
        for ci in chunks:
            yield from scan(ci)

    def run_interleaved(*gens):
        live = list(gens)
        while live:
            for g in list(live):
                try:
                    next(g)
                except StopIteration:
                    live.remove(g)

    groups = [list(range(i, min(i + DELTA_GROUP, n_chunks))) for i in range(0, n_chunks, DELTA_GROUP)]
    run_interleaved(intra(groups[0]))
    for gi, group in enumerate(groups):
        if gi + 1 < len(groups):
            run_interleaved(scans(group), intra(groups[gi + 1]))
        else:
            run_interleaved(scans(group))

    for h in range(heads):
        cols = slice(h * dv, (h + 1) * dv)
        oh = o_buf[:, cols]
        on = oh * lax.rsqrt(jnp.mean(oh * oh, axis=-1, keepdims=True) + NORM_EPS) * ng_ref[...]
        act[:, cols] = (on * _silu(z_ref[:, cols])).astype(BF16)
    out_ref[...] = h_ref[...] + _dot(act[...], wo_ref[...])


def _gdn_mixer(h, g, w_in, conv_w, a_log, dt_bias, norm_g, w_out):
    bsz, seq, d = h.shape
    heads = a_log.shape[0]
    taps, qkv_dim = conv_w.shape
    vd = w_out.shape[0]
    slabs = qkv_dim // LANES
    assert qkv_dim == 3 * vd and vd == heads * LANES, (qkv_dim, vd, heads)

    w_qkvz = w_in[:, :qkv_dim + vd].astype(BF16)
    w_ab = jnp.pad(w_in[:, qkv_dim + vd:], ((0, 0), (0, LANES - 2 * heads))).astype(BF16)
    alog = jnp.pad(a_log, (0, LANES - heads)).reshape(1, LANES)
    dtb = jnp.pad(dt_bias, (0, LANES - heads)).reshape(1, LANES)
    cw_b = jnp.broadcast_to(conv_w.reshape(taps, slabs, 1, LANES).transpose(1, 0, 2, 3),
                            (slabs, taps, SUBLANES, LANES))

    def head_major(ts):
        return pl.BlockSpec((None, heads, ts, LANES), lambda b, s: (b, 0, s, 0))

    ts = min(GDN_IN_TILE, seq)
    hm_shape = jax.ShapeDtypeStruct((bsz, heads, seq, LANES), F32)
    q, k, v, z, gates = pl.pallas_call(
        functools.partial(_gdn_in_kernel, heads=heads),
        out_shape=[hm_shape, hm_shape, hm_shape,
                   jax.ShapeDtypeStruct((bsz, seq, vd), F32),
                   jax.ShapeDtypeStruct((bsz, seq, LANES), F32)],
        grid=(bsz, seq // ts),
        in_specs=[_tile(ts, d), _full((1, d)), _full((d, qkv_dim + vd)), _full((d, LANES)),
                  _full((slabs, taps, SUBLANES, LANES)), _full((1, LANES)), _full((1, LANES))],
        out_specs=[head_major(ts), head_major(ts), head_major(ts), _tile(ts, vd), _tile(ts, LANES)],
        scratch_shapes=[pltpu.VMEM((SHORT_HALO + ts, LANES), F32)] * slabs,
        compiler_params=_params(),
        name="gdn_in",
    )(h, g.reshape(1, d), w_qkvz, w_ab, cw_b, alog, dtb)

    ts = min(DELTA_TILE, seq)
    return pl.pallas_call(
        functools.partial(_delta_kernel, heads=heads),
        out_shape=jax.ShapeDtypeStruct(h.shape, F32),
        grid=(bsz, seq // ts),
        in_specs=[head_major(ts), head_major(ts), head_major(ts), _tile(ts, LANES),
                  _tile(ts, vd), _tile(ts, d), _full((1, LANES)), _full((vd, d))],
        out_specs=_tile(ts, d),
        scratch_shapes=[pltpu.VMEM((heads, LANES, LANES), F32),
                        pltpu.VMEM((ts, vd), F32),
                        pltpu.VMEM((ts, vd), BF16)],
        compiler_params=_params(),
        name="delta_rule",
    )(q, k, v, gates, z, h, norm_g.reshape(1, LANES), w_out.astype(BF16))


def kernel(x, norm_mix_g, norm_ffn_g, final_norm_g, cv_w_pw1, cv_b_pw1, cv_w_dw, cv_b_dw, cv_ln_g, cv_ln_b, cv_w_pw2, cv_b_pw2, gdn_w_in, gdn_conv_w, gdn_a_log, gdn_dt_bias, gdn_norm_g, gdn_w_out, mlp_w1, mlp_w2):
    depth = norm_mix_g.shape[0]
    h = x
    for i in range(depth):
        j = i // 2
        if i % 2 == 0:
            h = _conv_mixer(h, norm_mix_g[i], cv_w_pw1[j], cv_b_pw1[j], cv_w_dw[j], cv_b_dw[j],
                            cv_ln_g[j], cv_ln_b[j], cv_w_pw2[j], cv_b_pw2[j])
        else:
            h = _gdn_mixer(h, norm_mix_g[i], gdn_w_in[j], gdn_conv_w[j], gdn_a_log[j],
                           gdn_dt_bias[j], gdn_norm_g[j], gdn_w_out[j])
        h = _mlp(h, norm_ffn_g[i], mlp_w1[i], mlp_w2[i], final_norm_g, i == depth - 1)
    return h
```

```python
import functools

import jax
import jax.numpy as jnp
from jax import lax
from jax.experimental import pallas as pl
from jax.experimental.pallas import tpu as pltpu

F32 = jnp.float32
BF16 = jnp.bfloat16
NORM_EPS = 1e-6
L2_EPS = 1e-6
CHUNK = 64
INV_BLOCK = 16
LANES = 128
SUBLANES = 8
VMEM_LIMIT_BYTES = 56 * 1024 * 1024

CONV_TILE = 512
MLP_TILE = 512
GDN_IN_TILE = 256
DELTA_TILE = 256
DELTA_GROUP = 4
CONV_HALO = 32
SHORT_HALO = 8
CONV_ROWS = 32
FF_CHUNK = 512


def _rms_norm(x, g):
    ms = jnp.mean(x * x, axis=-1, keepdims=True)
    return x * lax.rsqrt(ms + NORM_EPS) * g


def _silu(x):
    return x * jax.nn.sigmoid(x)


def _dot(a, b):
    return jnp.dot(a, b, preferred_element_type=F32)


def _dot_t(a, b):
    return lax.dot_general(a, b, (((1,), (1,)), ((), ())), preferred_element_type=F32)


def _tdot(a, b):
    return lax.dot_general(a, b, (((0,), (0,)), ((), ())), preferred_element_type=F32)


def _full(shape):
    return pl.BlockSpec(shape, lambda b, s: (0,) * len(shape))


def _tile(ts, d):
    return pl.BlockSpec((None, ts, d), lambda b, s: (b, s, 0))


def _params():
    return pltpu.CompilerParams(
        dimension_semantics=("arbitrary", "arbitrary"),
        vmem_limit_bytes=VMEM_LIMIT_BYTES)


def _conv_mixer_kernel(x_ref, g_ref, w1_ref, b1_ref, wdw_ref, bdw_ref, lng_ref, lnb_ref,
                       w2_ref, b2_ref, o_ref, *scratch):
    s = pl.program_id(1)
    ts, d = x_ref.shape
    slabs, taps = wdw_ref.shape[0], wdw_ref.shape[1]
    ubufs, cvos, act = scratch[:slabs], scratch[slabs:2 * slabs], scratch[2 * slabs]
    base = CONV_HALO - (taps - 1)
    stride = CONV_ROWS // SUBLANES

    @pl.when(s == 0)
    def _():
        for lt in range(slabs):
            ubufs[lt][0:CONV_HALO, :] = jnp.zeros((CONV_HALO, LANES), F32)

    hn = _rms_norm(x_ref[...], g_ref[...]).astype(BF16)
    for lt in range(slabs):
        cols = slice(2 * lt * LANES, 2 * (lt + 1) * LANES)
        u = _dot(hn, w1_ref[:, cols]) + b1_ref[:, cols]
        ubuf, cvo = ubufs[lt], cvos[lt]
        ubuf[CONV_HALO:CONV_HALO + ts, :] = u[:, :LANES] * jax.nn.sigmoid(u[:, LANES:])
        w = [wdw_ref[lt, k] for k in range(taps)]
        bias = bdw_ref[lt]
        for r0 in range(0, ts, CONV_ROWS):
            acc = [bias] * stride
            for m in range(taps + stride - 1):
                win = ubuf[pl.ds(r0 + base + m, SUBLANES, stride=stride), :]
                for j in range(stride):
                    if 0 <= m - j < taps:
                        acc[j] = acc[j] + w[m - j] * win
            for j in range(stride):
                cvo[pl.ds(r0 + j, SUBLANES, stride=stride), :] = acc[j]
        ubuf[0:CONV_HALO, :] = ubuf[ts:ts + CONV_HALO, :]

    def norm_rows(c, carry):
        for half in range(2):
            r0 = pl.multiple_of((2 * c + half) * CONV_ROWS, CONV_ROWS)
            xs = [cvos[lt][pl.ds(r0, CONV_ROWS), :] for lt in range(slabs)]
            tot = xs[0]
            for lt in range(1, slabs):
                tot = tot + xs[lt]
            mu = jnp.sum(tot, axis=-1, keepdims=True) * (1.0 / d)
            xc = [x - mu for x in xs]
            sq = xc[0] * xc[0]
            for lt in range(1, slabs):
                sq = sq + xc[lt] * xc[lt]
            inv = lax.rsqrt(jnp.sum(sq, axis=-1, keepdims=True) * (1.0 / d) + NORM_EPS)
            for lt in range(slabs):
                cols = slice(lt * LANES, (lt + 1) * LANES)
                y = xc[lt] * inv * lng_ref[:, cols] + lnb_ref[:, cols]
                act[pl.ds(r0, CONV_ROWS), cols] = _silu(y).astype(BF16)
        return carry

    lax.fori_loop(0, ts // (2 * CONV_ROWS), norm_rows, 0)
    o_ref[...] = x_ref[...] + _dot(act[...], w2_ref[...]) + b2_ref[...]


def _conv_mixer(h, g, w1, b1, wdw, bdw, lng, lnb, w2, b2):
    bsz, seq, d = h.shape
    ts = min(CONV_TILE, seq)
    taps = wdw.shape[0]
    slabs = d // LANES
    wdw_b = jnp.broadcast_to(wdw.reshape(taps, slabs, 1, LANES).transpose(1, 0, 2, 3),
                             (slabs, taps, SUBLANES, LANES))
    bdw_b = jnp.broadcast_to(bdw.reshape(slabs, 1, LANES), (slabs, SUBLANES, LANES))
    w1_g = w1.reshape(d, 2, slabs, LANES).transpose(0, 2, 1, 3).reshape(d, 2 * d).astype(BF16)
    b1_g = b1.reshape(2, slabs, LANES).transpose(1, 0, 2).reshape(1, 2 * d)
    return pl.pallas_call(
        _conv_mixer_kernel,
        out_shape=jax.ShapeDtypeStruct(h.shape, F32),
        grid=(bsz, seq // ts),
        in_specs=[_tile(ts, d), _full((1, d)), _full((d, 2 * d)), _full((1, 2 * d)),
                  _full((slabs, taps, SUBLANES, LANES)), _full((slabs, SUBLANES, LANES)),
                  _full((1, d)), _full((1, d)), _full((d, d)), _full((1, d))],
        out_specs=_tile(ts, d),
        scratch_shapes=([pltpu.VMEM((CONV_HALO + ts, LANES), F32)] * slabs
                        + [pltpu.VMEM((ts, LANES), F32)] * slabs
                        + [pltpu.VMEM((ts, d), BF16)]),
        compiler_params=_params(),
        name="conv_mixer",
    )(h, g.reshape(1, d), w1_g, b1_g, wdw_b, bdw_b,
      lng.reshape(1, d), lnb.reshape(1, d), w2.astype(BF16), b2.reshape(1, d))


def _mlp_kernel(h_ref, g_ref, w1_ref, w2_ref, fg_ref, o_ref, *, final_norm):
    d_ff = w1_ref.shape[1]
    hn = _rms_norm(h_ref[...], g_ref[...]).astype(BF16)
    acc = None
    for j in range(d_ff // FF_CHUNK):
        cols = slice(j * FF_CHUNK, (j + 1) * FF_CHUNK)
        a = jnp.maximum(_dot(hn, w1_ref[:, cols]), 0.0)
        p = _dot((a * a).astype(BF16), w2_ref[cols, :])
        acc = p if acc is None else acc + p
    y = h_ref[...] + acc
    if final_norm:
        y = _rms_norm(y, fg_ref[...])
    o_ref[...] = y


def _mlp(h, g, w1, w2, final_g, final_norm):
    bsz, seq, d = h.shape
    d_ff = w1.shape[1]
    ts = min(MLP_TILE, seq)
    return pl.pallas_call(
        functools.partial(_mlp_kernel, final_norm=final_norm),
        out_shape=jax.ShapeDtypeStruct(h.shape, F32),
        grid=(bsz, seq // ts),
        in_specs=[_tile(ts, d), _full((1, d)), _full((d, d_ff)), _full((d_ff, d)), _full((1, d))],
        out_specs=_tile(ts, d),
        compiler_params=_params(),
        name="mlp_final" if final_norm else "mlp",
    )(h, g.reshape(1, d), w1.astype(BF16), w2.astype(BF16), final_g.reshape(1, d))


def _run_interleaved(*gens):
    live = list(gens)
    while live:
        for g in list(live):
            try:
                next(g)
            except StopIteration:
                live.remove(g)


def _conv_layer_kernel(x_ref, g_ref, w1_ref, b1_ref, wdw_ref, bdw_ref, lng_ref, lnb_ref,
                       w2_ref, b2_ref, mg_ref, mw1_ref, mw2_ref, fg_ref, o_ref, *scratch,
                       final_norm):
    s = pl.program_id(1)
    ts, d = x_ref.shape
    slabs, taps = wdw_ref.shape[0], wdw_ref.shape[1]
    d_ff = mw1_ref.shape[1]
    ubufs, cvos = scratch[:slabs], scratch[slabs:2 * slabs]
    act, h1 = scratch[2 * slabs], scratch[2 * slabs + 1]
    base = CONV_HALO - (taps - 1)
    stride = CONV_ROWS // SUBLANES
    slot = s % 2

    @pl.when(s == 0)
    def _():
        for lt in range(slabs):
            ubufs[lt][0:CONV_HALO, :] = jnp.zeros((CONV_HALO, LANES), F32)
        h1[1] = jnp.zeros((ts, d), F32)

    def mixer():
        hn = _rms_norm(x_ref[...], g_ref[...]).astype(BF16)
        for lt in range(slabs):
            cols = slice(2 * lt * LANES, 2 * (lt + 1) * LANES)
            u = _dot(hn, w1_ref[:, cols]) + b1_ref[:, cols]
            yield
            ubuf, cvo = ubufs[lt], cvos[lt]
            ubuf[CONV_HALO:CONV_HALO + ts, :] = u[:, :LANES] * jax.nn.sigmoid(u[:, LANES:])
            w = [wdw_ref[lt, k] for k in range(taps)]
            bias = bdw_ref[lt]
            for r0 in range(0, ts, CONV_ROWS):
                acc = [bias] * stride
                for m in range(taps + stride - 1):
                    win = ubuf[pl.ds(r0 + base + m, SUBLANES, stride=stride), :]
                    for j in range(stride):
                        if 0 <= m - j < taps:
                            acc[j] = acc[j] + w[m - j] * win
                for j in range(stride):
                    cvo[pl.ds(r0 + j, SUBLANES, stride=stride), :] = acc[j]
            ubuf[0:CONV_HALO, :] = ubuf[ts:ts + CONV_HALO, :]
        for r0 in range(0, ts, CONV_ROWS):
            if r0 % (ts // 4) == 0:
                yield
            xs = [cvos[lt][r0:r0 + CONV_ROWS, :] for lt in range(slabs)]
            tot = xs[0]
            for lt in range(1, slabs):
                tot = tot + xs[lt]
            mu = jnp.sum(tot, axis=-1, keepdims=True) * (1.0 / d)
            xc = [x - mu for x in xs]
            sq = xc[0] * xc[0]
            for lt in range(1, slabs):
                sq = sq + xc[lt] * xc[lt]
            inv = lax.rsqrt(jnp.sum(sq, axis=-1, keepdims=True) * (1.0 / d) + NORM_EPS)
            for lt in range(slabs):
                cols = slice(lt * LANES, (lt + 1) * LANES)
                y = xc[lt] * inv * lng_ref[:, cols] + lnb_ref[:, cols]
                act[r0:r0 + CONV_ROWS, cols] = _silu(y).astype(BF16)
        yield
        h1[slot] = x_ref[...] + _dot(act[...], w2_ref[...]) + b2_ref[...]

    def mlp():
        hn = _rms_norm(h1[1 - slot], mg_ref[...]).astype(BF16)
        acc = None
        for j in range(d_ff // FF_CHUNK):
            cols = slice(j * FF_CHUNK, (j + 1) * FF_CHUNK)
            a = jnp.maximum(_dot(hn, mw1_ref[:, cols]), 0.0)
            yield
            p = _dot((a * a).astype(BF16), mw2_ref[cols, :])
            acc = p if acc is None else acc + p
            yield
        y = h1[1 - slot] + acc
        if final_norm:
            y = _rms_norm(y, fg_ref[...])
        o_ref[...] = y

    _run_interleaved(mixer(), mlp())


def _conv_layer(h, g, w1, b1, wdw, bdw, lng, lnb, w2, b2, mg, mw1, mw2, final_g, final_norm):
    bsz, seq, d = h.shape
    ts = min(CONV_TILE, seq)
    n_tiles = seq // ts
    taps = wdw.shape[0]
    slabs = d // LANES
    d_ff = mw1.shape[1]
    wdw_b = jnp.broadcast_to(wdw.reshape(taps, slabs, 1, LANES).transpose(1, 0, 2, 3),
                             (slabs, taps, SUBLANES, LANES))
    bdw_b = jnp.broadcast_to(bdw.reshape(slabs, 1, LANES), (slabs, SUBLANES, LANES))
    w1_g = w1.reshape(d, 2, slabs, LANES).transpose(0, 2, 1, 3).reshape(d, 2 * d).astype(BF16)
    b1_g = b1.reshape(2, slabs, LANES).transpose(1, 0, 2).reshape(1, 2 * d)

    def const(shape):
        return pl.BlockSpec(shape, lambda b, s: (0,) * len(shape), pipeline_mode=pl.Buffered(1))

    return pl.pallas_call(
        functools.partial(_conv_layer_kernel, final_norm=final_norm),
        out_shape=jax.ShapeDtypeStruct(h.shape, F32),
        grid=(bsz, n_tiles + 1),
        in_specs=[pl.BlockSpec((None, ts, d), lambda b, s: (b, jnp.minimum(s, n_tiles - 1), 0)),
                  const((1, d)), const((d, 2 * d)), const((1, 2 * d)),
                  const((slabs, taps, SUBLANES, LANES)), const((slabs, SUBLANES, LANES)),
                  const((1, d)), const((1, d)), const((d, d)), const((1, d)),
                  const((1, d)), const((d, d_ff)), const((d_ff, d)), const((1, d))],
        out_specs=pl.BlockSpec((None, ts, d), lambda b, s: (b, jnp.maximum(s - 1, 0), 0)),
        scratch_shapes=([pltpu.VMEM((CONV_HALO + ts, LANES), F32)] * slabs
                        + [pltpu.VMEM((ts, LANES), F32)] * slabs
                        + [pltpu.VMEM((ts, d), BF16), pltpu.VMEM((2, ts, d), F32)]),
        compiler_params=_params(),
        name="conv_layer",
    )(h, g.reshape(1, d), w1_g, b1_g, wdw_b, bdw_b, lng.reshape(1, d), lnb.reshape(1, d),
      w2.astype(BF16), b2.reshape(1, d), mg.reshape(1, d), mw1.astype(BF16), mw2.astype(BF16),
      final_g.reshape(1, d))


def _gdn_in_kernel(h_ref, g_ref, wq_ref, wab_ref, cw_ref, alog_ref, dtb_ref,
                   q_ref, k_ref, v_ref, z_ref, gt_ref, *cbufs, heads):
    s = pl.program_id(1)
    ts, d = h_ref.shape
    slabs, taps = cw_ref.shape[0], cw_ref.shape[1]
    qkv_dim = slabs * LANES
    base = SHORT_HALO - (taps - 1)
    stride = CONV_ROWS // SUBLANES
    q_scale = LANES ** -0.5

    @pl.when(s == 0)
    def _():
        for sl in range(slabs):
            cbufs[sl][0:SHORT_HALO, :] = jnp.zeros((SHORT_HALO, LANES), F32)

    hn = _rms_norm(h_ref[...], g_ref[...]).astype(BF16)

    ab = _dot(hn, wab_ref[...])
    pre = ab + dtb_ref[...]
    softplus = jnp.maximum(pre, 0.0) + jnp.log1p(jnp.exp(-jnp.abs(pre)))
    lane = lax.broadcasted_iota(jnp.int32, ab.shape, 1)
    gt_ref[...] = jnp.where(lane < heads, -jnp.exp(alog_ref[...]) * softplus, jax.nn.sigmoid(ab))

    for blk in range(slabs // 2):
        proj = _dot(hn, wq_ref[:, 2 * blk * LANES:2 * (blk + 1) * LANES])
        for sl in (2 * blk, 2 * blk + 1):
            cbuf = cbufs[sl]
            cbuf[SHORT_HALO:SHORT_HALO + ts, :] = proj[:, (sl % 2) * LANES:(sl % 2 + 1) * LANES]
            w = [cw_ref[sl, t] for t in range(taps)]
            for r0 in range(0, ts, CONV_ROWS):
                win = [cbuf[pl.ds(r0 + base + m, SUBLANES, stride=stride), :]
                       for m in range(taps + stride - 1)]
                for j in range(stride):
                    acc = w[0] * win[j]
                    for t in range(1, taps):
                        acc = acc + w[t] * win[j + t]
                    y = _silu(acc)
                    dst = pl.ds(r0 + j, SUBLANES, stride=stride)
                    if sl < heads:
                        q_ref[sl, dst, :] = (
                            y * lax.rsqrt(jnp.sum(y * y, axis=-1, keepdims=True) + L2_EPS) * q_scale)
                    elif sl < 2 * heads:
                        k_ref[sl - heads, dst, :] = (
                            y * lax.rsqrt(jnp.sum(y * y, axis=-1, keepdims=True) + L2_EPS))
                    else:
                        v_ref[sl - 2 * heads, dst, :] = y
            cbuf[0:SHORT_HALO, :] = cbuf[ts:ts + SHORT_HALO, :]
    z_ref[...] = _dot(hn, wq_ref[:, qkv_dim:])


def _delta_kernel(q_ref, k_ref, v_ref, gt_ref, z_ref, h_ref, ng_ref, wo_ref, out_ref,
                  state, o_buf, act, *, heads):
    s = pl.program_id(1)
    ts = q_ref.shape[1]
    dv = v_ref.shape[2]
    c = CHUNK
    n_chunks = ts // c

    @pl.when(s == 0)
    def _():
        state[...] = jnp.zeros(state.shape, F32)

    row = lax.broadcasted_iota(jnp.int32, (c, 2 * c), 0)
    lane = lax.broadcasted_iota(jnp.int32, (c, 2 * c), 1)
    col = jnp.where(lane < c, lane, lane - c)
    low = lane < c
    low_row = low[:1]
    low_tall = lax.broadcasted_iota(jnp.int32, (2 * c, 2 * c), 1) < c
    causal = row >= col
    eye = (col == row).astype(F32)
    diag_block = (row > col) & (row // INV_BLOCK == col // INV_BLOCK)
    merge_sizes = []
    b = INV_BLOCK
    while b < c:
        merge_sizes.append(b)
        b *= 2
    merge_masks = [(row // (2 * b) == col // (2 * b)) & (row // b == col // b + 1)
                   for b in merge_sizes]
    n_rounds = (INV_BLOCK - 1).bit_length()
    tril = (lax.broadcasted_iota(jnp.int32, (c, c), 0)
            >= lax.broadcasted_iota(jnp.int32, (c, c), 1)).astype(F32)
    sel = (lax.broadcasted_iota(jnp.int32, (heads, LANES), 0)
           == lax.broadcasted_iota(jnp.int32, (heads, LANES), 1)).astype(F32)
    prepared = {}

    def side_by_side(a, b):
        return jnp.concatenate([a, b], axis=1)

    def block_diag(a, b):
        return jnp.concatenate([side_by_side(a, jnp.zeros_like(b)),
                                side_by_side(jnp.zeros_like(a), b)], axis=0)

    def packed_block_diag(x):
        keep_a = low if x.shape[0] == c else low_tall
        zero = jnp.zeros_like(x)
        return jnp.concatenate([jnp.where(keep_a, x, zero), jnp.where(keep_a, zero, x)], axis=0)

    pairs = [(h, h + 1) for h in range(0, heads, 2)]

    def intra(chunks):
        gates, gcum, gcum_t = {}, {}, {}
        for ci in chunks:
            gates[ci] = gt_ref[ci * c:(ci + 1) * c, :]
            gcum[ci] = jnp.dot(tril, gates[ci], preferred_element_type=F32,
                               precision=lax.Precision.HIGHEST)
            gcum_t[ci] = lax.dot_general(
                sel, jnp.concatenate([gcum[ci], gcum[ci]], axis=0), (((1,), (1,)), ((), ())),
                preferred_element_type=F32, precision=lax.Precision.HIGHEST)
        yield
        duos = [(ci, ha, hb) for ci in chunks for ha, hb in pairs]
        x1, decay, kdec_t16, rhs16, qg16, g_last = {}, {}, {}, {}, {}, {}
        for duo in duos:
            ci, ha, hb = duo
            rows = slice(ci * c, (ci + 1) * c)
            lhs, kh_pair = [], []
            for h in (ha, hb):
                g_col = gcum[ci][:, h:h + 1]
                beta = gates[ci][:, heads + h:heads + h + 1]
                qh = q_ref[h, rows, :]
                kh = k_ref[h, rows, :]
                vh = v_ref[h, rows, :]
                kb = kh * beta
                e_col = jnp.exp(g_col)
                lhs.append(jnp.concatenate([kb.astype(BF16), qh.astype(BF16)], axis=0))
                kh_pair.append(kh)
                rhs16[ci, h] = jnp.concatenate([vh * beta, kb * e_col], axis=-1).astype(BF16)
                qg16[ci, h] = (qh * e_col).astype(BF16)
                g_last[ci, h] = gcum[ci][c - 1:c, h:h + 1]
            g_col = jnp.where(low, gcum[ci][:, ha:ha + 1], gcum[ci][:, hb:hb + 1])
            g_row = jnp.where(low_row, gcum_t[ci][ha:ha + 1, :], gcum_t[ci][hb:hb + 1, :])
            g_end = jnp.where(low_row, g_last[ci, ha], g_last[ci, hb])
            decay[duo] = jnp.exp(jnp.where(causal, g_col - g_row, -jnp.inf))
            k_t = jnp.concatenate(kh_pair, axis=0).T
            kdec_t16[duo] = (k_t * jnp.exp(g_end - g_row)).astype(BF16)
            x1[duo] = _dot(side_by_side(*lhs), packed_block_diag(k_t.astype(BF16)))
        yield
        inv, power, lower16, qk16 = {}, {}, {}, {}
        for duo in duos:
            lower = x1[duo][:c] * decay[duo]
            lower16[duo] = lower.astype(BF16)
            qk16[duo] = (x1[duo][c:] * decay[duo]).astype(BF16)
            inv[duo] = eye
            power[duo] = jnp.where(diag_block, -lower, 0.0)
        for r in range(n_rounds):
            prod = {}
            for duo in duos:
                rhs = packed_block_diag(inv[duo].astype(BF16))
                if r + 1 < n_rounds:
                    rhs = side_by_side(rhs, packed_block_diag(power[duo].astype(BF16)))
                prod[duo] = _dot(power[duo].astype(BF16), rhs)
            yield
            for duo in duos:
                inv[duo] = inv[duo] + prod[duo][:, :2 * c]
                if r + 1 < n_rounds:
                    power[duo] = prod[duo][:, 2 * c:]
        for mask in merge_masks:
            tmp = {}
            for duo in duos:
                corner16 = jnp.where(mask, lower16[duo], jnp.zeros_like(lower16[duo]))
                tmp[duo] = _dot(corner16, packed_block_diag(inv[duo].astype(BF16)))
            yield
            upd = {}
            for duo in duos:
                upd[duo] = _dot(inv[duo].astype(BF16), packed_block_diag(tmp[duo].astype(BF16)))
            yield
            inv = {duo: inv[duo] - upd[duo] for duo in duos}
        sol = {}
        for duo in duos:
            ci, ha, hb = duo
            sol[duo] = _dot(inv[duo].astype(BF16), block_diag(rhs16[ci, ha], rhs16[ci, hb]))
        yield
        for duo in duos:
            ci, ha, hb = duo
            width = sol[duo].shape[1] // 2
            halves = (sol[duo][:, :width], sol[duo][:, width:])
            prepared[duo] = dict(
                u=[x[:, :dv] for x in halves],
                lhs_proj=side_by_side(*[
                    jnp.concatenate([x[:, dv:].astype(BF16), qg16[ci, h]], axis=0)
                    for x, h in zip(halves, (ha, hb))]),
                lhs_intra=jnp.concatenate([qk16[duo], kdec_t16[duo]], axis=0),
                g_last=[g_last[ci, ha], g_last[ci, hb]])

    def scan(ci):
        rows = slice(ci * c, (ci + 1) * c)
        p = [prepared.pop((ci, ha, hb)) for ha, hb in pairs]
        proj = []
        for (ha, hb), pp in zip(pairs, p):
            proj.append(_dot(pp["lhs_proj"],
                             block_diag(state[ha].astype(BF16), state[hb].astype(BF16))))
        yield
        intra_out = []
        for pp, pj in zip(p, proj):
            va16 = (pp["u"][0] - pj[:c, :dv]).astype(BF16)
            vb16 = (pp["u"][1] - pj[:c, dv:]).astype(BF16)
            intra_out.append(_dot(pp["lhs_intra"], block_diag(va16, vb16)))
        yield
        for (ha, hb), pp, pj, io in zip(pairs, p, proj, intra_out):
            o_buf[rows, ha * dv:(hb + 1) * dv] = pj[c:] + io[:c]
            state[ha] = state[ha] * jnp.exp(pp["g_last"][0]) + io[c:, :dv]
            state[hb] = state[hb] * jnp.exp(pp["g_last"][1]) + io[c:, dv:]
        yield

    def scans(chunks):
        for ci in chunks:
            yield from scan(ci)

    def run_interleaved(*gens):
        live = list(gens)
        while live:
            for g in list(live):
                try:
                    next(g)
                except StopIteration:
                    live.remove(g)

    groups = [list(range(i, min(i + DELTA_GROUP, n_chunks))) for i in range(0, n_chunks, DELTA_GROUP)]
    run_interleaved(intra(groups[0]))
    for gi, group in enumerate(groups):
        if gi + 1 < len(groups):
            run_interleaved(scans(group), intra(groups[gi + 1]))
        else:
            run_interleaved(scans(group))

    for h in range(heads):
        cols = slice(h * dv, (h + 1) * dv)
        oh = o_buf[:, cols]
        on = oh * lax.rsqrt(jnp.mean(oh * oh, axis=-1, keepdims=True) + NORM_EPS) * ng_ref[...]
        act[:, cols] = (on * _silu(z_ref[:, cols])).astype(BF16)
    out_ref[...] = h_ref[...] + _dot(act[...], wo_ref[...])


def _gdn_mixer(h, g, w_in, conv_w, a_log, dt_bias, norm_g, w_out):
    bsz, seq, d = h.shape
    heads = a_log.shape[0]
    taps, qkv_dim = conv_w.shape
    vd = w_out.shape[0]
    slabs = qkv_dim // LANES
    assert qkv_dim == 3 * vd and vd == heads * LANES, (qkv_dim, vd, heads)

    w_qkvz = w_in[:, :qkv_dim + vd].astype(BF16)
    w_ab = jnp.pad(w_in[:, qkv_dim + vd:], ((0, 0), (0, LANES - 2 * heads))).astype(BF16)
    alog = jnp.pad(a_log, (0, LANES - heads)).reshape(1, LANES)
    dtb = jnp.pad(dt_bias, (0, LANES - heads)).reshape(1, LANES)
    cw_b = jnp.broadcast_to(conv_w.reshape(taps, slabs, 1, LANES).transpose(1, 0, 2, 3),
                            (slabs, taps, SUBLANES, LANES))

    def head_major(ts):
        return pl.BlockSpec((None, heads, ts, LANES), lambda b, s: (b, 0, s, 0))

    ts = min(GDN_IN_TILE, seq)
    hm_shape = jax.ShapeDtypeStruct((bsz, heads, seq, LANES), F32)
    q, k, v, z, gates = pl.pallas_call(
        functools.partial(_gdn_in_kernel, heads=heads),
        out_shape=[hm_shape, hm_shape, hm_shape,
                   jax.ShapeDtypeStruct((bsz, seq, vd), F32),
                   jax.ShapeDtypeStruct((bsz, seq, LANES), F32)],
        grid=(bsz, seq // ts),
        in_specs=[_tile(ts, d), _full((1, d)), _full((d, qkv_dim + vd)), _full((d, LANES)),
                  _full((slabs, taps, SUBLANES, LANES)), _full((1, LANES)), _full((1, LANES))],
        out_specs=[head_major(ts), head_major(ts), head_major(ts), _tile(ts, vd), _tile(ts, LANES)],
        scratch_shapes=[pltpu.VMEM((SHORT_HALO + ts, LANES), F32)] * slabs,
        compiler_params=_params(),
        name="gdn_in",
    )(h, g.reshape(1, d), w_qkvz, w_ab, cw_b, alog, dtb)

    ts = min(DELTA_TILE, seq)
    return pl.pallas_call(
        functools.partial(_delta_kernel, heads=heads),
        out_shape=jax.ShapeDtypeStruct(h.shape, F32),
        grid=(bsz, seq // ts),
        in_specs=[head_major(ts), head_major(ts), head_major(ts), _tile(ts, LANES),
                  _tile(ts, vd), _tile(ts, d), _full((1, LANES)), _full((vd, d))],
        out_specs=_tile(ts, d),
        scratch_shapes=[pltpu.VMEM((heads, LANES, LANES), F32),
                        pltpu.VMEM((ts, vd), F32),
                        pltpu.VMEM((ts, vd), BF16)],
        compiler_params=_params(),
        name="delta_rule",
    )(q, k, v, gates, z, h, norm_g.reshape(1, LANES), w_out.astype(BF16))


def kernel(x, norm_mix_g, norm_ffn_g, final_norm_g, cv_w_pw1, cv_b_pw1, cv_w_dw, cv_b_dw, cv_ln_g, cv_ln_b, cv_w_pw2, cv_b_pw2, gdn_w_in, gdn_conv_w, gdn_a_log, gdn_dt_bias, gdn_norm_g, gdn_w_out, mlp_w1, mlp_w2):
    depth = norm_mix_g.shape[0]
    h = x
    for i in range(depth):
        j = i // 2
        last = i == depth - 1
        if i % 2 == 0:
            h = _conv_layer(h, norm_mix_g[i], cv_w_pw1[j], cv_b_pw1[j], cv_w_dw[j], cv_b_dw[j],
                            cv_ln_g[j], cv_ln_b[j], cv_w_pw2[j], cv_b_pw2[j],
                            norm_ffn_g[i], mlp_w1[i], mlp_w2[i], final_norm_g, last)
        else:
            h = _gdn_mixer(h, norm_mix_g[i], gdn_w_in[j], gdn_conv_w[j], gdn_a_log[j],
                           gdn_dt_bias[j], gdn_norm_g[j], gdn_w_out[j])
            h = _mlp(h, norm_ffn_g[i], mlp_w1[i], mlp_w2[i], final_norm_g, last)
    return h
```

```python
import functools

import jax
import jax.numpy as jnp
from jax import lax
from jax.experimental import pallas as pl
from jax.experimental.pallas import tpu as pltpu

F32 = jnp.float32
BF16 = jnp.bfloat16
NORM_EPS = 1e-6
L2_EPS = 1e-6
CHUNK = 64
INV_BLOCK = 16
LANES = 128
SUBLANES = 8
VMEM_LIMIT_BYTES = 56 * 1024 * 1024

CONV_TILE = 256
MLP_TILE = 512
GDN_TILE = 512
DELTA_TILE = 512
DELTA_GROUP = 4
CONV_HALO = 32
SHORT_HALO = 8
CONV_ROWS = 32
FF_CHUNK = 512


def _rms_norm(x, g):
    ms = jnp.mean(x * x, axis=-1, keepdims=True)
    return x * lax.rsqrt(ms + NORM_EPS) * g


def _silu(x):
    return x * jax.nn.sigmoid(x)


def _dot(a, b):
    return jnp.dot(a, b, preferred_element_type=F32)


def _full(shape):
    return pl.BlockSpec(shape, lambda b, s: (0,) * len(shape))


def _tile(ts, d):
    return pl.BlockSpec((None, ts, d), lambda b, s: (b, s, 0))


def _params():
    return pltpu.CompilerParams(
        dimension_semantics=("arbitrary", "arbitrary"),
        vmem_limit_bytes=VMEM_LIMIT_BYTES)


def _run_interleaved(*gens):
    live = list(gens)
    while live:
        for g in list(live):
            try:
                next(g)
            except StopIteration:
                live.remove(g)


def _alternate(*gens):
    live = list(gens)
    while live:
        for g in list(live):
            try:
                next(g)
            except StopIteration:
                live.remove(g)
                continue
            yield


def _mlp_kernel(h_ref, g_ref, w1_ref, w2_ref, fg_ref, o_ref, *, final_norm):
    d_ff = w1_ref.shape[1]
    hn = _rms_norm(h_ref[...], g_ref[...]).astype(BF16)
    acc = None
    for j in range(d_ff // FF_CHUNK):
        cols = slice(j * FF_CHUNK, (j + 1) * FF_CHUNK)
        a = jnp.maximum(_dot(hn, w1_ref[:, cols]), 0.0)
        p = _dot((a * a).astype(BF16), w2_ref[cols, :])
        acc = p if acc is None else acc + p
    y = h_ref[...] + acc
    if final_norm:
        y = _rms_norm(y, fg_ref[...])
    o_ref[...] = y


def _mlp(h, g, w1, w2, final_g, final_norm):
    bsz, seq, d = h.shape
    d_ff = w1.shape[1]
    ts = min(MLP_TILE, seq)
    return pl.pallas_call(
        functools.partial(_mlp_kernel, final_norm=final_norm),
        out_shape=jax.ShapeDtypeStruct(h.shape, F32),
        grid=(bsz, seq // ts),
        in_specs=[_tile(ts, d), _full((1, d)), _full((d, d_ff)), _full((d_ff, d)), _full((1, d))],
        out_specs=_tile(ts, d),
        compiler_params=_params(),
        name="mlp_final" if final_norm else "mlp",
    )(h, g.reshape(1, d), w1.astype(BF16), w2.astype(BF16), final_g.reshape(1, d))


def _conv_layer_kernel(x_ref, g_ref, w1_ref, b1_ref, wdw_ref, bdw_ref, lng_ref, lnb_ref,
                       w2_ref, b2_ref, mg_ref, mw1_ref, mw2_ref, fg_ref, o_ref, *scratch,
                       final_norm):
    s = pl.program_id(1)
    ts, d = x_ref.shape
    slabs, taps = wdw_ref.shape[0], wdw_ref.shape[1]
    d_ff = mw1_ref.shape[1]
    ubufs, cvos = scratch[:slabs], scratch[slabs:2 * slabs]
    act, h1 = scratch[2 * slabs], scratch[2 * slabs + 1]
    base = CONV_HALO - (taps - 1)
    stride = CONV_ROWS // SUBLANES
    slot = s % 2

    @pl.when(s == 0)
    def _():
        for lt in range(slabs):
            ubufs[lt][0:CONV_HALO, :] = jnp.zeros((CONV_HALO, LANES), F32)
        h1[1] = jnp.zeros((ts, d), F32)

    def mixer():
        hn = _rms_norm(x_ref[...], g_ref[...]).astype(BF16)
        for lt in range(slabs):
            cols = slice(2 * lt * LANES, 2 * (lt + 1) * LANES)
            u = _dot(hn, w1_ref[:, cols]) + b1_ref[:, cols]
            yield
            ubuf, cvo = ubufs[lt], cvos[lt]
            ubuf[CONV_HALO:CONV_HALO + ts, :] = u[:, :LANES] * jax.nn.sigmoid(u[:, LANES:])
            w = [wdw_ref[lt, k] for k in range(taps)]
            bias = bdw_ref[lt]
            for r0 in range(0, ts, CONV_ROWS):
                acc = [bias] * stride
                for m in range(taps + stride - 1):
                    win = ubuf[pl.ds(r0 + base + m, SUBLANES, stride=stride), :]
                    for j in range(stride):
                        if 0 <= m - j < taps:
                            acc[j] = acc[j] + w[m - j] * win
                for j in range(stride):
                    cvo[pl.ds(r0 + j, SUBLANES, stride=stride), :] = acc[j]
            ubuf[0:CONV_HALO, :] = ubuf[ts:ts + CONV_HALO, :]
        for r0 in range(0, ts, CONV_ROWS):
            if r0 % (ts // 4) == 0:
                yield
            xs = [cvos[lt][r0:r0 + CONV_ROWS, :] for lt in range(slabs)]
            tot = xs[0]
            for lt in range(1, slabs):
                tot = tot + xs[lt]
            mu = jnp.sum(tot, axis=-1, keepdims=True) * (1.0 / d)
            xc = [x - mu for x in xs]
            sq = xc[0] * xc[0]
            for lt in range(1, slabs):
                sq = sq + xc[lt] * xc[lt]
            inv = lax.rsqrt(jnp.sum(sq, axis=-1, keepdims=True) * (1.0 / d) + NORM_EPS)
            for lt in range(slabs):
                cols = slice(lt * LANES, (lt + 1) * LANES)
                y = xc[lt] * inv * lng_ref[:, cols] + lnb_ref[:, cols]
                act[r0:r0 + CONV_ROWS, cols] = _silu(y).astype(BF16)
        yield
        h1[slot] = x_ref[...] + _dot(act[...], w2_ref[...]) + b2_ref[...]

    def mlp():
        hn = _rms_norm(h1[1 - slot], mg_ref[...]).astype(BF16)
        acc = None
        for j in range(d_ff // FF_CHUNK):
            cols = slice(j * FF_CHUNK, (j + 1) * FF_CHUNK)
            a = jnp.maximum(_dot(hn, mw1_ref[:, cols]), 0.0)
            yield
            p = _dot((a * a).astype(BF16), mw2_ref[cols, :])
            acc = p if acc is None else acc + p
            yield
        y = h1[1 - slot] + acc
        if final_norm:
            y = _rms_norm(y, fg_ref[...])
        o_ref[...] = y

    _run_interleaved(mixer(), mlp())


def _conv_layer(h, g, w1, b1, wdw, bdw, lng, lnb, w2, b2, mg, mw1, mw2, final_g, final_norm):
    bsz, seq, d = h.shape
    ts = min(CONV_TILE, seq)
    n_tiles = seq // ts
    taps = wdw.shape[0]
    slabs = d // LANES
    d_ff = mw1.shape[1]
    wdw_b = jnp.broadcast_to(wdw.reshape(taps, slabs, 1, LANES).transpose(1, 0, 2, 3),
                             (slabs, taps, SUBLANES, LANES))
    bdw_b = jnp.broadcast_to(bdw.reshape(slabs, 1, LANES), (slabs, SUBLANES, LANES))
    w1_g = w1.reshape(d, 2, slabs, LANES).transpose(0, 2, 1, 3).reshape(d, 2 * d).astype(BF16)
    b1_g = b1.reshape(2, slabs, LANES).transpose(1, 0, 2).reshape(1, 2 * d)

    def const(shape):
        return pl.BlockSpec(shape, lambda b, s: (0,) * len(shape), pipeline_mode=pl.Buffered(1))

    return pl.pallas_call(
        functools.partial(_conv_layer_kernel, final_norm=final_norm),
        out_shape=jax.ShapeDtypeStruct(h.shape, F32),
        grid=(bsz, n_tiles + 1),
        in_specs=[pl.BlockSpec((None, ts, d), lambda b, s: (b, jnp.minimum(s, n_tiles - 1), 0)),
                  const((1, d)), const((d, 2 * d)), const((1, 2 * d)),
                  const((slabs, taps, SUBLANES, LANES)), const((slabs, SUBLANES, LANES)),
                  const((1, d)), const((1, d)), const((d, d)), const((1, d)),
                  const((1, d)), const((d, d_ff)), const((d_ff, d)), const((1, d))],
        out_specs=pl.BlockSpec((None, ts, d), lambda b, s: (b, jnp.maximum(s - 1, 0), 0)),
        scratch_shapes=([pltpu.VMEM((CONV_HALO + ts, LANES), F32)] * slabs
                        + [pltpu.VMEM((ts, LANES), F32)] * slabs
                        + [pltpu.VMEM((ts, d), BF16), pltpu.VMEM((2, ts, d), F32)]),
        compiler_params=_params(),
        name="conv_layer",
    )(h, g.reshape(1, d), w1_g, b1_g, wdw_b, bdw_b, lng.reshape(1, d), lnb.reshape(1, d),
      w2.astype(BF16), b2.reshape(1, d), mg.reshape(1, d), mw1.astype(BF16), mw2.astype(BF16),
      final_g.reshape(1, d))


def _gdn_in_kernel(h_ref, g_ref, wq_ref, wab_ref, cw_ref, alog_ref, dtb_ref,
                   q_ref, k_ref, v_ref, z_ref, gt_ref, *cbufs, heads):
    s = pl.program_id(1)
    ts, d = h_ref.shape
    slabs, taps = cw_ref.shape[0], cw_ref.shape[1]
    qkv_dim = slabs * LANES
    base = SHORT_HALO - (taps - 1)
    stride = CONV_ROWS // SUBLANES
    q_scale = LANES ** -0.5

    @pl.when(s == 0)
    def _():
        for cbuf in cbufs:
            cbuf[0:SHORT_HALO, :] = jnp.zeros((SHORT_HALO, LANES), F32)

    hn = _rms_norm(h_ref[...], g_ref[...]).astype(BF16)

    ab = _dot(hn, wab_ref[...])
    pre = ab + dtb_ref[...]
    softplus = jnp.maximum(pre, 0.0) + jnp.log1p(jnp.exp(-jnp.abs(pre)))
    lane = lax.broadcasted_iota(jnp.int32, ab.shape, 1)
    gt_ref[...] = jnp.where(lane < heads, -jnp.exp(alog_ref[...]) * softplus, jax.nn.sigmoid(ab))

    for blk in range(slabs // 2):
        proj = _dot(hn, wq_ref[:, 2 * blk * LANES:2 * (blk + 1) * LANES])
        for sl in (2 * blk, 2 * blk + 1):
            cbuf = cbufs[sl]
            cbuf[SHORT_HALO:SHORT_HALO + ts, :] = proj[:, (sl % 2) * LANES:(sl % 2 + 1) * LANES]
            w = [cw_ref[sl, t] for t in range(taps)]
            for r0 in range(0, ts, CONV_ROWS):
                win = [cbuf[pl.ds(r0 + base + m, SUBLANES, stride=stride), :]
                       for m in range(taps + stride - 1)]
                for j in range(stride):
                    acc = w[0] * win[j]
                    for t in range(1, taps):
                        acc = acc + w[t] * win[j + t]
                    y = _silu(acc)
                    dst = pl.ds(r0 + j, SUBLANES, stride=stride)
                    if sl < heads:
                        q_ref[sl, dst, :] = (
                            y * lax.rsqrt(jnp.sum(y * y, axis=-1, keepdims=True) + L2_EPS) * q_scale)
                    elif sl < 2 * heads:
                        k_ref[sl - heads, dst, :] = (
                            y * lax.rsqrt(jnp.sum(y * y, axis=-1, keepdims=True) + L2_EPS))
                    else:
                        v_ref[sl - 2 * heads, dst, :] = y
            cbuf[0:SHORT_HALO, :] = cbuf[ts:ts + SHORT_HALO, :]
    z_ref[...] = _dot(hn, wq_ref[:, qkv_dim:])


def _delta_kernel(q_ref, k_ref, v_ref, gt_ref, z_ref, h_ref, ng_ref, wo_ref, out_ref,
                  state, o_buf, act, *, heads):
    s = pl.program_id(1)
    ts = q_ref.shape[1]
    dv = v_ref.shape[2]
    c = CHUNK
    n_chunks = ts // c

    @pl.when(s == 0)
    def _():
        state[...] = jnp.zeros(state.shape, F32)

    row = lax.broadcasted_iota(jnp.int32, (c, 2 * c), 0)
    lane = lax.broadcasted_iota(jnp.int32, (c, 2 * c), 1)
    col = jnp.where(lane < c, lane, lane - c)
    low = lane < c
    low_row = low[:1]
    low_tall = lax.broadcasted_iota(jnp.int32, (2 * c, 2 * c), 1) < c
    causal = row >= col
    eye = (col == row).astype(F32)
    diag_block = (row > col) & (row // INV_BLOCK == col // INV_BLOCK)
    merge_sizes = []
    b = INV_BLOCK
    while b < c:
        merge_sizes.append(b)
        b *= 2
    merge_masks = [(row // (2 * b) == col // (2 * b)) & (row // b == col // b + 1)
                   for b in merge_sizes]
    n_rounds = (INV_BLOCK - 1).bit_length()
    tril = (lax.broadcasted_iota(jnp.int32, (c, c), 0)
            >= lax.broadcasted_iota(jnp.int32, (c, c), 1)).astype(F32)
    sel = (lax.broadcasted_iota(jnp.int32, (heads, LANES), 0)
           == lax.broadcasted_iota(jnp.int32, (heads, LANES), 1)).astype(F32)
    prepared = {}

    def side_by_side(a, b):
        return jnp.concatenate([a, b], axis=1)

    def block_diag(a, b):
        return jnp.concatenate([side_by_side(a, jnp.zeros_like(b)),
                                side_by_side(jnp.zeros_like(a), b)], axis=0)

    def packed_block_diag(x):
        keep_a = low if x.shape[0] == c else low_tall
        zero = jnp.zeros_like(x)
        return jnp.concatenate([jnp.where(keep_a, x, zero), jnp.where(keep_a, zero, x)], axis=0)

    pairs = [(h, h + 1) for h in range(0, heads, 2)]

    def intra(chunks):
        gates, gcum, gcum_t = {}, {}, {}
        for ci in chunks:
            gates[ci] = gt_ref[ci * c:(ci + 1) * c, :]
            gcum[ci] = jnp.dot(tril, gates[ci], preferred_element_type=F32,
                               precision=lax.Precision.HIGHEST)
            gcum_t[ci] = lax.dot_general(
                sel, jnp.concatenate([gcum[ci], gcum[ci]], axis=0), (((1,), (1,)), ((), ())),
                preferred_element_type=F32, precision=lax.Precision.HIGHEST)
        yield
        duos = [(ci, ha, hb) for ci in chunks for ha, hb in pairs]
        x1, decay, kdec_t16, rhs16, qg16, g_last = {}, {}, {}, {}, {}, {}
        for duo in duos:
            ci, ha, hb = duo
            rows = slice(ci * c, (ci + 1) * c)
            lhs, kh_pair = [], []
            for h in (ha, hb):
                g_col = gcum[ci][:, h:h + 1]
                beta = gates[ci][:, heads + h:heads + h + 1]
                qh = q_ref[h, rows, :]
                kh = k_ref[h, rows, :]
                vh = v_ref[h, rows, :]
                kb = kh * beta
                e_col = jnp.exp(g_col)
                lhs.append(jnp.concatenate([kb.astype(BF16), qh.astype(BF16)], axis=0))
                kh_pair.append(kh)
                rhs16[ci, h] = jnp.concatenate([vh * beta, kb * e_col], axis=-1).astype(BF16)
                qg16[ci, h] = (qh * e_col).astype(BF16)
                g_last[ci, h] = gcum[ci][c - 1:c, h:h + 1]
            g_col = jnp.where(low, gcum[ci][:, ha:ha + 1], gcum[ci][:, hb:hb + 1])
            g_row = jnp.where(low_row, gcum_t[ci][ha:ha + 1, :], gcum_t[ci][hb:hb + 1, :])
            g_end = jnp.where(low_row, g_last[ci, ha], g_last[ci, hb])
            decay[duo] = jnp.exp(jnp.where(causal, g_col - g_row, -jnp.inf))
            k_t = jnp.concatenate(kh_pair, axis=0).T
            kdec_t16[duo] = (k_t * jnp.exp(g_end - g_row)).astype(BF16)
            x1[duo] = _dot(side_by_side(*lhs), packed_block_diag(k_t.astype(BF16)))
        yield
        inv, power, lower16, qk16 = {}, {}, {}, {}
        for duo in duos:
            lower = x1[duo][:c] * decay[duo]
            lower16[duo] = lower.astype(BF16)
            qk16[duo] = (x1[duo][c:] * decay[duo]).astype(BF16)
            inv[duo] = eye
            power[duo] = jnp.where(diag_block, -lower, 0.0)
        for r in range(n_rounds):
            prod = {}
            for duo in duos:
                rhs = packed_block_diag(inv[duo].astype(BF16))
                if r + 1 < n_rounds:
                    rhs = side_by_side(rhs, packed_block_diag(power[duo].astype(BF16)))
                prod[duo] = _dot(power[duo].astype(BF16), rhs)
            yield
            for duo in duos:
                inv[duo] = inv[duo] + prod[duo][:, :2 * c]
                if r + 1 < n_rounds:
                    power[duo] = prod[duo][:, 2 * c:]
        for mask in merge_masks:
            tmp = {}
            for duo in duos:
                corner16 = jnp.where(mask, lower16[duo], jnp.zeros_like(lower16[duo]))
                tmp[duo] = _dot(corner16, packed_block_diag(inv[duo].astype(BF16)))
            yield
            upd = {}
            for duo in duos:
                upd[duo] = _dot(inv[duo].astype(BF16), packed_block_diag(tmp[duo].astype(BF16)))
            yield
            inv = {duo: inv[duo] - upd[duo] for duo in duos}
        sol = {}
        for duo in duos:
            ci, ha, hb = duo
            sol[duo] = _dot(inv[duo].astype(BF16), block_diag(rhs16[ci, ha], rhs16[ci, hb]))
        yield
        for duo in duos:
            ci, ha, hb = duo
            width = sol[duo].shape[1] // 2
            halves = (sol[duo][:, :width], sol[duo][:, width:])
            prepared[duo] = dict(
                u=[x[:, :dv] for x in halves],
                lhs_proj=side_by_side(*[
                    jnp.concatenate([x[:, dv:].astype(BF16), qg16[ci, h]], axis=0)
                    for x, h in zip(halves, (ha, hb))]),
                lhs_intra=jnp.concatenate([qk16[duo], kdec_t16[duo]], axis=0),
                g_last=[g_last[ci, ha], g_last[ci, hb]])

    def scan(ci):
        rows = slice(ci * c, (ci + 1) * c)
        p = [prepared.pop((ci, ha, hb)) for ha, hb in pairs]
        proj = []
        for (ha, hb), pp in zip(pairs, p):
            proj.append(_dot(pp["lhs_proj"],
                             block_diag(state[ha].astype(BF16), state[hb].astype(BF16))))
        yield
        intra_out = []
        for pp, pj in zip(p, proj):
            va16 = (pp["u"][0] - pj[:c, :dv]).astype(BF16)
            vb16 = (pp["u"][1] - pj[:c, dv:]).astype(BF16)
            intra_out.append(_dot(pp["lhs_intra"], block_diag(va16, vb16)))
        yield
        for (ha, hb), pp, pj, io in zip(pairs, p, proj, intra_out):
            o_buf[rows, ha * dv:(hb + 1) * dv] = pj[c:] + io[:c]
            state[ha] = state[ha] * jnp.exp(pp["g_last"][0]) + io[c:, :dv]
            state[hb] = state[hb] * jnp.exp(pp["g_last"][1]) + io[c:, dv:]
        yield

    def scans(chunks):
        for ci in chunks:
            yield from scan(ci)

    def delta_rule():
        groups = [list(range(i, min(i + DELTA_GROUP, n_chunks)))
                  for i in range(0, n_chunks, DELTA_GROUP)]
        yield from intra(groups[0])
        for gi, group in enumerate(groups):
            if gi + 1 < len(groups):
                yield from _alternate(scans(group), intra(groups[gi + 1]))
            else:
                yield from scans(group)

    _run_interleaved(delta_rule())
    for h in range(heads):
        cols = slice(h * dv, (h + 1) * dv)
        oh = o_buf[:, cols]
        on = oh * lax.rsqrt(jnp.mean(oh * oh, axis=-1, keepdims=True) + NORM_EPS) * ng_ref[...]
        act[:, cols] = (on * _silu(z_ref[:, cols])).astype(BF16)
    out_ref[...] = h_ref[...] + _dot(act[...], wo_ref[...])


def _gdn_mixer(h, g, w_in, conv_w, a_log, dt_bias, norm_g, w_out):
    bsz, seq, d = h.shape
    heads = a_log.shape[0]
    taps, qkv_dim = conv_w.shape
    vd = w_out.shape[0]
    slabs = qkv_dim // LANES
    assert qkv_dim == 3 * vd and vd == heads * LANES, (qkv_dim, vd, heads)

    w_qkvz = w_in[:, :qkv_dim + vd].astype(BF16)
    w_ab = jnp.pad(w_in[:, qkv_dim + vd:], ((0, 0), (0, LANES - 2 * heads))).astype(BF16)
    alog = jnp.pad(a_log, (0, LANES - heads)).reshape(1, LANES)
    dtb = jnp.pad(dt_bias, (0, LANES - heads)).reshape(1, LANES)
    cw_b = jnp.broadcast_to(conv_w.reshape(taps, slabs, 1, LANES).transpose(1, 0, 2, 3),
                            (slabs, taps, SUBLANES, LANES))

    def head_major(ts):
        return pl.BlockSpec((None, heads, ts, LANES), lambda b, s: (b, 0, s, 0))

    ts = min(GDN_TILE, seq)
    hm_shape = jax.ShapeDtypeStruct((bsz, heads, seq, LANES), F32)
    q, k, v, z, gates = pl.pallas_call(
        functools.partial(_gdn_in_kernel, heads=heads),
        out_shape=[hm_shape, hm_shape, hm_shape,
                   jax.ShapeDtypeStruct((bsz, seq, vd), F32),
                   jax.ShapeDtypeStruct((bsz, seq, LANES), F32)],
        grid=(bsz, seq // ts),
        in_specs=[_tile(ts, d), _full((1, d)), _full((d, qkv_dim + vd)), _full((d, LANES)),
                  _full((slabs, taps, SUBLANES, LANES)), _full((1, LANES)), _full((1, LANES))],
        out_specs=[head_major(ts), head_major(ts), head_major(ts), _tile(ts, vd), _tile(ts, LANES)],
        scratch_shapes=[pltpu.VMEM((SHORT_HALO + ts, LANES), F32)] * slabs,
        compiler_params=_params(),
        name="gdn_in",
    )(h, g.reshape(1, d), w_qkvz, w_ab, cw_b, alog, dtb)

    ts = min(DELTA_TILE, seq)
    return pl.pallas_call(
        functools.partial(_delta_kernel, heads=heads),
        out_shape=jax.ShapeDtypeStruct(h.shape, F32),
        grid=(bsz, seq // ts),
        in_specs=[head_major(ts), head_major(ts), head_major(ts), _tile(ts, LANES),
                  _tile(ts, vd), _tile(ts, d), _full((1, LANES)), _full((vd, d))],
        out_specs=_tile(ts, d),
        scratch_shapes=[pltpu.VMEM((heads, LANES, LANES), F32),
                        pltpu.VMEM((ts, vd), F32),
                        pltpu.VMEM((ts, vd), BF16)],
        compiler_params=_params(),
        name="delta_rule",
    )(q, k, v, gates, z, h, norm_g.reshape(1, LANES), w_out.astype(BF16))


def kernel(x, norm_mix_g, norm_ffn_g, final_norm_g, cv_w_pw1, cv_b_pw1, cv_w_dw, cv_b_dw, cv_ln_g, cv_ln_b, cv_w_pw2, cv_b_pw2, gdn_w_in, gdn_conv_w, gdn_a_log, gdn_dt_bias, gdn_norm_g, gdn_w_out, mlp_w1, mlp_w2):
    depth = norm_mix_g.shape[0]
    h = x
    for i in range(depth):
        j = i // 2
        last = i == depth - 1
        if i % 2 == 0:
            h = _conv_layer(h, norm_mix_g[i], cv_w_pw1[j], cv_b_pw1[j], cv_w_dw[j], cv_b_dw[j],
                            cv_ln_g[j], cv_ln_b[j], cv_w_pw2[j], cv_b_pw2[j],
                            norm_ffn_g[i], mlp_w1[i], mlp_w2[i], final_norm_g, last)
        else:
            h = _gdn_mixer(h, norm_mix_g[i], gdn_w_in[j], gdn_conv_w[j], gdn_a_log[j],
                           gdn_dt_bias[j], gdn_norm_g[j], gdn_w_out[j])
            h = _mlp(h, norm_ffn_g[i], mlp_w1[i], mlp_w2[i], final_norm_g, last)
    return h
```

```python
import functools

import jax
import jax.numpy as jnp
from jax import lax
from jax.experimental import pallas as pl
from jax.experimental.pallas import tpu as pltpu

F32 = jnp.float32
BF16 = jnp.bfloat16
NORM_EPS = 1e-6
L2_EPS = 1e-6
CHUNK = 64
INV_BLOCK = 16
LANES = 128
SUBLANES = 8
VMEM_LIMIT_BYTES = 56 * 1024 * 1024

CONV_TILE = 512
MLP_TILE = 512
GDN_TILE = 512
DELTA_TILE = 512
DELTA_GROUP = 4
CONV_HALO = 32
SHORT_HALO = 8
CONV_ROWS = 32
FF_CHUNK = 512


def _rms_norm(x, g):
    ms = jnp.mean(x * x, axis=-1, keepdims=True)
    return x * lax.rsqrt(ms + NORM_EPS) * g


def _silu(x):
    return x * jax.nn.sigmoid(x)


def _dot(a, b):
    return jnp.dot(a, b, preferred_element_type=F32)


def _full(shape):
    return pl.BlockSpec(shape, lambda b, s: (0,) * len(shape))


def _tile(ts, d):
    return pl.BlockSpec((None, ts, d), lambda b, s: (b, s, 0))


def _params():
    return pltpu.CompilerParams(
        dimension_semantics=("arbitrary", "arbitrary"),
        vmem_limit_bytes=VMEM_LIMIT_BYTES)


def _run_interleaved(*gens):
    live = list(gens)
    while live:
        for g in list(live):
            try:
                next(g)
            except StopIteration:
                live.remove(g)


def _alternate(*gens):
    live = list(gens)
    while live:
        for g in list(live):
            try:
                next(g)
            except StopIteration:
                live.remove(g)
                continue
            yield


def _mlp_kernel(h_ref, g_ref, w1_ref, w2_ref, fg_ref, o_ref, *, final_norm):
    d_ff = w1_ref.shape[1]
    hn = _rms_norm(h_ref[...], g_ref[...]).astype(BF16)
    acc = None
    for j in range(d_ff // FF_CHUNK):
        cols = slice(j * FF_CHUNK, (j + 1) * FF_CHUNK)
        a = jnp.maximum(_dot(hn, w1_ref[:, cols]), 0.0)
        p = _dot((a * a).astype(BF16), w2_ref[cols, :])
        acc = p if acc is None else acc + p
    y = h_ref[...] + acc
    if final_norm:
        y = _rms_norm(y, fg_ref[...])
    o_ref[...] = y


def _mlp(h, g, w1, w2, final_g, final_norm):
    bsz, seq, d = h.shape
    d_ff = w1.shape[1]
    ts = min(MLP_TILE, seq)
    return pl.pallas_call(
        functools.partial(_mlp_kernel, final_norm=final_norm),
        out_shape=jax.ShapeDtypeStruct(h.shape, F32),
        grid=(bsz, seq // ts),
        in_specs=[_tile(ts, d), _full((1, d)), _full((d, d_ff)), _full((d_ff, d)), _full((1, d))],
        out_specs=_tile(ts, d),
        compiler_params=_params(),
        name="mlp_final" if final_norm else "mlp",
    )(h, g.reshape(1, d), w1.astype(BF16), w2.astype(BF16), final_g.reshape(1, d))


def _conv_layer_kernel(x_ref, g_ref, w1_ref, b1_ref, wdw_ref, bdw_ref, lng_ref, lnb_ref,
                       w2_ref, b2_ref, mg_ref, mw1_ref, mw2_ref, fg_ref, o_ref, *scratch,
                       final_norm, tiles_per_row):
    t = pl.program_id(0)
    ts, d = x_ref.shape
    slabs, taps = wdw_ref.shape[0], wdw_ref.shape[1]
    d_ff = mw1_ref.shape[1]
    ubufs, cvos = scratch[:slabs], scratch[slabs:2 * slabs]
    act, h1 = scratch[2 * slabs], scratch[2 * slabs + 1]
    base = CONV_HALO - (taps - 1)
    stride = CONV_ROWS // SUBLANES
    slot = t % 2

    @pl.when(t % tiles_per_row == 0)
    def _():
        for lt in range(slabs):
            ubufs[lt][0:CONV_HALO, :] = jnp.zeros((CONV_HALO, LANES), F32)

    @pl.when(t == 0)
    def _():
        h1[1] = jnp.zeros((ts, d), F32)

    def mixer():
        hn = _rms_norm(x_ref[...], g_ref[...]).astype(BF16)
        for lt in range(slabs):
            cols = slice(2 * lt * LANES, 2 * (lt + 1) * LANES)
            u = _dot(hn, w1_ref[:, cols]) + b1_ref[:, cols]
            yield
            ubuf, cvo = ubufs[lt], cvos[lt]
            ubuf[CONV_HALO:CONV_HALO + ts, :] = u[:, :LANES] * jax.nn.sigmoid(u[:, LANES:])
            w = [wdw_ref[lt, k] for k in range(taps)]
            bias = bdw_ref[lt]
            for r0 in range(0, ts, CONV_ROWS):
                acc = [bias] * stride
                for m in range(taps + stride - 1):
                    win = ubuf[pl.ds(r0 + base + m, SUBLANES, stride=stride), :]
                    for j in range(stride):
                        if 0 <= m - j < taps:
                            acc[j] = acc[j] + w[m - j] * win
                for j in range(stride):
                    cvo[pl.ds(r0 + j, SUBLANES, stride=stride), :] = acc[j]
            ubuf[0:CONV_HALO, :] = ubuf[ts:ts + CONV_HALO, :]
        for r0 in range(0, ts, CONV_ROWS):
            if r0 % (ts // 4) == 0:
                yield
            xs = [cvos[lt][r0:r0 + CONV_ROWS, :] for lt in range(slabs)]
            tot = xs[0]
            for lt in range(1, slabs):
                tot = tot + xs[lt]
            mu = jnp.sum(tot, axis=-1, keepdims=True) * (1.0 / d)
            xc = [x - mu for x in xs]
            sq = xc[0] * xc[0]
            for lt in range(1, slabs):
                sq = sq + xc[lt] * xc[lt]
            inv = lax.rsqrt(jnp.sum(sq, axis=-1, keepdims=True) * (1.0 / d) + NORM_EPS)
            for lt in range(slabs):
                cols = slice(lt * LANES, (lt + 1) * LANES)
                y = xc[lt] * inv * lng_ref[:, cols] + lnb_ref[:, cols]
                act[r0:r0 + CONV_ROWS, cols] = _silu(y).astype(BF16)
        yield
        h1[slot] = x_ref[...] + _dot(act[...], w2_ref[...]) + b2_ref[...]

    def mlp():
        hn = _rms_norm(h1[1 - slot], mg_ref[...]).astype(BF16)
        acc = None
        for j in range(d_ff // FF_CHUNK):
            cols = slice(j * FF_CHUNK, (j + 1) * FF_CHUNK)
            a = jnp.maximum(_dot(hn, mw1_ref[:, cols]), 0.0)
            yield
            p = _dot((a * a).astype(BF16), mw2_ref[cols, :])
            acc = p if acc is None else acc + p
            yield
        y = h1[1 - slot] + acc
        if final_norm:
            y = _rms_norm(y, fg_ref[...])
        o_ref[...] = y

    _run_interleaved(mixer(), mlp())


def _conv_layer(h, g, w1, b1, wdw, bdw, lng, lnb, w2, b2, mg, mw1, mw2, final_g, final_norm):
    bsz, seq, d = h.shape
    ts = min(CONV_TILE, seq)
    n_tiles = seq // ts
    taps = wdw.shape[0]
    slabs = d // LANES
    d_ff = mw1.shape[1]
    wdw_b = jnp.broadcast_to(wdw.reshape(taps, slabs, 1, LANES).transpose(1, 0, 2, 3),
                             (slabs, taps, SUBLANES, LANES))
    bdw_b = jnp.broadcast_to(bdw.reshape(slabs, 1, LANES), (slabs, SUBLANES, LANES))
    w1_g = w1.reshape(d, 2, slabs, LANES).transpose(0, 2, 1, 3).reshape(d, 2 * d).astype(BF16)
    b1_g = b1.reshape(2, slabs, LANES).transpose(1, 0, 2).reshape(1, 2 * d)

    def const(shape):
        return pl.BlockSpec(shape, lambda t: (0,) * len(shape), pipeline_mode=pl.Buffered(1))

    total = bsz * n_tiles
    out = pl.pallas_call(
        functools.partial(_conv_layer_kernel, final_norm=final_norm, tiles_per_row=n_tiles),
        out_shape=jax.ShapeDtypeStruct((total, ts, d), F32),
        grid=(total + 1,),
        in_specs=[pl.BlockSpec((None, ts, d), lambda t: (jnp.minimum(t, total - 1), 0, 0)),
                  const((1, d)), const((d, 2 * d)), const((1, 2 * d)),
                  const((slabs, taps, SUBLANES, LANES)), const((slabs, SUBLANES, LANES)),
                  const((1, d)), const((1, d)), const((d, d)), const((1, d)),
                  const((1, d)), const((d, d_ff)), const((d_ff, d)), const((1, d))],
        out_specs=pl.BlockSpec((None, ts, d), lambda t: (jnp.maximum(t - 1, 0), 0, 0)),
        scratch_shapes=([pltpu.VMEM((CONV_HALO + ts, LANES), F32)] * slabs
                        + [pltpu.VMEM((ts, LANES), F32)] * slabs
                        + [pltpu.VMEM((ts, d), BF16), pltpu.VMEM((2, ts, d), F32)]),
        compiler_params=pltpu.CompilerParams(dimension_semantics=("arbitrary",),
                                             vmem_limit_bytes=VMEM_LIMIT_BYTES),
        name="conv_layer",
    )(h.reshape(total, ts, d), g.reshape(1, d), w1_g, b1_g, wdw_b, bdw_b, lng.reshape(1, d),
      lnb.reshape(1, d), w2.astype(BF16), b2.reshape(1, d), mg.reshape(1, d),
      mw1.astype(BF16), mw2.astype(BF16), final_g.reshape(1, d))
    return out.reshape(bsz, seq, d)


def _gdn_in_kernel(h_ref, g_ref, wq_ref, wab_ref, cw_ref, alog_ref, dtb_ref,
                   q_ref, k_ref, v_ref, z_ref, gt_ref, *cbufs, heads):
    s = pl.program_id(1)
    ts, d = h_ref.shape
    slabs, taps = cw_ref.shape[0], cw_ref.shape[1]
    qkv_dim = slabs * LANES
    base = SHORT_HALO - (taps - 1)
    stride = CONV_ROWS // SUBLANES
    q_scale = LANES ** -0.5

    @pl.when(s == 0)
    def _():
        for cbuf in cbufs:
            cbuf[0:SHORT_HALO, :] = jnp.zeros((SHORT_HALO, LANES), F32)

    hn = _rms_norm(h_ref[...], g_ref[...]).astype(BF16)

    ab = _dot(hn, wab_ref[...])
    pre = ab + dtb_ref[...]
    softplus = jnp.maximum(pre, 0.0) + jnp.log1p(jnp.exp(-jnp.abs(pre)))
    lane = lax.broadcasted_iota(jnp.int32, ab.shape, 1)
    gt_ref[...] = jnp.where(lane < heads, -jnp.exp(alog_ref[...]) * softplus, jax.nn.sigmoid(ab))

    for blk in range(slabs // 2):
        proj = _dot(hn, wq_ref[:, 2 * blk * LANES:2 * (blk + 1) * LANES])
        for sl in (2 * blk, 2 * blk + 1):
            cbuf = cbufs[sl]
            cbuf[SHORT_HALO:SHORT_HALO + ts, :] = proj[:, (sl % 2) * LANES:(sl % 2 + 1) * LANES]
            w = [cw_ref[sl, t] for t in range(taps)]
            for r0 in range(0, ts, CONV_ROWS):
                win = [cbuf[pl.ds(r0 + base + m, SUBLANES, stride=stride), :]
                       for m in range(taps + stride - 1)]
                for j in range(stride):
                    acc = w[0] * win[j]
                    for t in range(1, taps):
                        acc = acc + w[t] * win[j + t]
                    y = _silu(acc)
                    dst = pl.ds(r0 + j, SUBLANES, stride=stride)
                    if sl < heads:
                        q_ref[sl, dst, :] = (
                            y * lax.rsqrt(jnp.sum(y * y, axis=-1, keepdims=True) + L2_EPS) * q_scale)
                    elif sl < 2 * heads:
                        k_ref[sl - heads, dst, :] = (
                            y * lax.rsqrt(jnp.sum(y * y, axis=-1, keepdims=True) + L2_EPS))
                    else:
                        v_ref[sl - 2 * heads, dst, :] = y
            cbuf[0:SHORT_HALO, :] = cbuf[ts:ts + SHORT_HALO, :]
    z_ref[...] = _dot(hn, wq_ref[:, qkv_dim:])


def _delta_kernel(q_ref, k_ref, v_ref, gt_ref, z_ref, h_ref, ng_ref, wo_ref, out_ref,
                  state, o_buf, act, *, heads):
    s = pl.program_id(1)
    ts = q_ref.shape[1]
    dv = v_ref.shape[2]
    c = CHUNK
    n_chunks = ts // c

    @pl.when(s == 0)
    def _():
        state[...] = jnp.zeros(state.shape, F32)

    row = lax.broadcasted_iota(jnp.int32, (c, 2 * c), 0)
    lane = lax.broadcasted_iota(jnp.int32, (c, 2 * c), 1)
    col = jnp.where(lane < c, lane, lane - c)
    low = lane < c
    low_row = low[:1]
    low_tall = lax.broadcasted_iota(jnp.int32, (2 * c, 2 * c), 1) < c
    causal = row >= col
    eye = (col == row).astype(F32)
    diag_block = (row > col) & (row // INV_BLOCK == col // INV_BLOCK)
    merge_sizes = []
    b = INV_BLOCK
    while b < c:
        merge_sizes.append(b)
        b *= 2
    merge_masks = [(row // (2 * b) == col // (2 * b)) & (row // b == col // b + 1)
                   for b in merge_sizes]
    n_rounds = (INV_BLOCK - 1).bit_length()
    tril = (lax.broadcasted_iota(jnp.int32, (c, c), 0)
            >= lax.broadcasted_iota(jnp.int32, (c, c), 1)).astype(F32)
    sel = (lax.broadcasted_iota(jnp.int32, (heads, LANES), 0)
           == lax.broadcasted_iota(jnp.int32, (heads, LANES), 1)).astype(F32)
    prepared = {}

    def side_by_side(a, b):
        return jnp.concatenate([a, b], axis=1)

    def block_diag(a, b):
        return jnp.concatenate([side_by_side(a, jnp.zeros_like(b)),
                                side_by_side(jnp.zeros_like(a), b)], axis=0)

    def packed_block_diag(x):
        keep_a = low if x.shape[0] == c else low_tall
        zero = jnp.zeros_like(x)
        return jnp.concatenate([jnp.where(keep_a, x, zero), jnp.where(keep_a, zero, x)], axis=0)

    pairs = [(h, h + 1) for h in range(0, heads, 2)]

    def intra(chunks):
        gates, gcum, gcum_t = {}, {}, {}
        for ci in chunks:
            gates[ci] = gt_ref[ci * c:(ci + 1) * c, :]
            gcum[ci] = jnp.dot(tril, gates[ci], preferred_element_type=F32,
                               precision=lax.Precision.HIGHEST)
            gcum_t[ci] = lax.dot_general(
                sel, jnp.concatenate([gcum[ci], gcum[ci]], axis=0), (((1,), (1,)), ((), ())),
                preferred_element_type=F32, precision=lax.Precision.HIGHEST)
        yield
        duos = [(ci, ha, hb) for ci in chunks for ha, hb in pairs]
        x1, decay, kdec_t16, rhs16, qg16, g_last = {}, {}, {}, {}, {}, {}
        for duo in duos:
            ci, ha, hb = duo
            rows = slice(ci * c, (ci + 1) * c)
            lhs, kh_pair = [], []
            for h in (ha, hb):
                g_col = gcum[ci][:, h:h + 1]
                beta = gates[ci][:, heads + h:heads + h + 1]
                qh = q_ref[h, rows, :]
                kh = k_ref[h, rows, :]
                vh = v_ref[h, rows, :]
                kb = kh * beta
                e_col = jnp.exp(g_col)
                lhs.append(jnp.concatenate([kb.astype(BF16), qh.astype(BF16)], axis=0))
                kh_pair.append(kh)
                rhs16[ci, h] = jnp.concatenate([vh * beta, kb * e_col], axis=-1).astype(BF16)
                qg16[ci, h] = (qh * e_col).astype(BF16)
                g_last[ci, h] = gcum[ci][c - 1:c, h:h + 1]
            g_col = jnp.where(low, gcum[ci][:, ha:ha + 1], gcum[ci][:, hb:hb + 1])
            g_row = jnp.where(low_row, gcum_t[ci][ha:ha + 1, :], gcum_t[ci][hb:hb + 1, :])
            g_end = jnp.where(low_row, g_last[ci, ha], g_last[ci, hb])
            decay[duo] = jnp.exp(jnp.where(causal, g_col - g_row, -jnp.inf))
            k_t = jnp.concatenate(kh_pair, axis=0).T
            kdec_t16[duo] = (k_t * jnp.exp(g_end - g_row)).astype(BF16)
            x1[duo] = _dot(side_by_side(*lhs), packed_block_diag(k_t.astype(BF16)))
        yield
        inv, power, lower16, qk16 = {}, {}, {}, {}
        for duo in duos:
            lower = x1[duo][:c] * decay[duo]
            lower16[duo] = lower.astype(BF16)
            qk16[duo] = (x1[duo][c:] * decay[duo]).astype(BF16)
            inv[duo] = eye
            power[duo] = jnp.where(diag_block, -lower, 0.0)
        for r in range(n_rounds):
            prod = {}
            for duo in duos:
                rhs = packed_block_diag(inv[duo].astype(BF16))
                if r + 1 < n_rounds:
                    rhs = side_by_side(rhs, packed_block_diag(power[duo].astype(BF16)))
                prod[duo] = _dot(power[duo].astype(BF16), rhs)
            yield
            for duo in duos:
                inv[duo] = inv[duo] + prod[duo][:, :2 * c]
                if r + 1 < n_rounds:
                    power[duo] = prod[duo][:, 2 * c:]
        for mask in merge_masks:
            tmp = {}
            for duo in duos:
                corner16 = jnp.where(mask, lower16[duo], jnp.zeros_like(lower16[duo]))
                tmp[duo] = _dot(corner16, packed_block_diag(inv[duo].astype(BF16)))
            yield
            upd = {}
            for duo in duos:
                upd[duo] = _dot(inv[duo].astype(BF16), packed_block_diag(tmp[duo].astype(BF16)))
            yield
            inv = {duo: inv[duo] - upd[duo] for duo in duos}
        sol = {}
        for duo in duos:
            ci, ha, hb = duo
            sol[duo] = _dot(inv[duo].astype(BF16), block_diag(rhs16[ci, ha], rhs16[ci, hb]))
        yield
        for duo in duos:
            ci, ha, hb = duo
            width = sol[duo].shape[1] // 2
            halves = (sol[duo][:, :width], sol[duo][:, width:])
            prepared[duo] = dict(
                u=[x[:, :dv] for x in halves],
                lhs_proj=side_by_side(*[
                    jnp.concatenate([x[:, dv:].astype(BF16), qg16[ci, h]], axis=0)
                    for x, h in zip(halves, (ha, hb))]),
                lhs_intra=jnp.concatenate([qk16[duo], kdec_t16[duo]], axis=0),
                g_last=[g_last[ci, ha], g_last[ci, hb]])

    def scan(ci):
        rows = slice(ci * c, (ci + 1) * c)
        p = [prepared.pop((ci, ha, hb)) for ha, hb in pairs]
        proj = []
        for (ha, hb), pp in zip(pairs, p):
            proj.append(_dot(pp["lhs_proj"],
                             block_diag(state[ha].astype(BF16), state[hb].astype(BF16))))
        yield
        intra_out = []
        for pp, pj in zip(p, proj):
            va16 = (pp["u"][0] - pj[:c, :dv]).astype(BF16)
            vb16 = (pp["u"][1] - pj[:c, dv:]).astype(BF16)
            intra_out.append(_dot(pp["lhs_intra"], block_diag(va16, vb16)))
        yield
        for (ha, hb), pp, pj, io in zip(pairs, p, proj, intra_out):
            o_buf[rows, ha * dv:(hb + 1) * dv] = pj[c:] + io[:c]
            state[ha] = state[ha] * jnp.exp(pp["g_last"][0]) + io[c:, :dv]
            state[hb] = state[hb] * jnp.exp(pp["g_last"][1]) + io[c:, dv:]
        yield

    def scans(chunks):
        for ci in chunks:
            yield from scan(ci)

    def delta_rule():
        groups = [list(range(i, min(i + DELTA_GROUP, n_chunks)))
                  for i in range(0, n_chunks, DELTA_GROUP)]
        yield from intra(groups[0])
        for gi, group in enumerate(groups):
            if gi + 1 < len(groups):
                yield from _alternate(scans(group), intra(groups[gi + 1]))
            else:
                yield from scans(group)

    _run_interleaved(delta_rule())
    for h in range(heads):
        cols = slice(h * dv, (h + 1) * dv)
        oh = o_buf[:, cols]
        on = oh * lax.rsqrt(jnp.mean(oh * oh, axis=-1, keepdims=True) + NORM_EPS) * ng_ref[...]
        act[:, cols] = (on * _silu(z_ref[:, cols])).astype(BF16)
    out_ref[...] = h_ref[...] + _dot(act[...], wo_ref[...])


def _gdn_mixer(h, g, w_in, conv_w, a_log, dt_bias, norm_g, w_out):
    bsz, seq, d = h.shape
    heads = a_log.shape[0]
    taps, qkv_dim = conv_w.shape
    vd = w_out.shape[0]
    slabs = qkv_dim // LANES
    assert qkv_dim == 3 * vd and vd == heads * LANES, (qkv_dim, vd, heads)

    w_qkvz = w_in[:, :qkv_dim + vd].astype(BF16)
    w_ab = jnp.pad(w_in[:, qkv_dim + vd:], ((0, 0), (0, LANES - 2 * heads))).astype(BF16)
    alog = jnp.pad(a_log, (0, LANES - heads)).reshape(1, LANES)
    dtb = jnp.pad(dt_bias, (0, LANES - heads)).reshape(1, LANES)
    cw_b = jnp.broadcast_to(conv_w.reshape(taps, slabs, 1, LANES).transpose(1, 0, 2, 3),
                            (slabs, taps, SUBLANES, LANES))

    def head_major(ts):
        return pl.BlockSpec((None, heads, ts, LANES), lambda b, s: (b, 0, s, 0))

    ts = min(GDN_TILE, seq)
    hm_shape = jax.ShapeDtypeStruct((bsz, heads, seq, LANES), F32)
    q, k, v, z, gates = pl.pallas_call(
        functools.partial(_gdn_in_kernel, heads=heads),
        out_shape=[hm_shape, hm_shape, hm_shape,
                   jax.ShapeDtypeStruct((bsz, seq, vd), F32),
                   jax.ShapeDtypeStruct((bsz, seq, LANES), F32)],
        grid=(bsz, seq // ts),
        in_specs=[_tile(ts, d), _full((1, d)), _full((d, qkv_dim + vd)), _full((d, LANES)),
                  _full((slabs, taps, SUBLANES, LANES)), _full((1, LANES)), _full((1, LANES))],
        out_specs=[head_major(ts), head_major(ts), head_major(ts), _tile(ts, vd), _tile(ts, LANES)],
        scratch_shapes=[pltpu.VMEM((SHORT_HALO + ts, LANES), F32)] * slabs,
        compiler_params=_params(),
        name="gdn_in",
    )(h, g.reshape(1, d), w_qkvz, w_ab, cw_b, alog, dtb)

    ts = min(DELTA_TILE, seq)
    return pl.pallas_call(
        functools.partial(_delta_kernel, heads=heads),
        out_shape=jax.ShapeDtypeStruct(h.shape, F32),
        grid=(bsz, seq // ts),
        in_specs=[head_major(ts), head_major(ts), head_major(ts), _tile(ts, LANES),
                  _tile(ts, vd), _tile(ts, d), _full((1, LANES)), _full((vd, d))],
        out_specs=_tile(ts, d),
        scratch_shapes=[pltpu.VMEM((heads, LANES, LANES), F32),
                        pltpu.VMEM((ts, vd), F32),
                        pltpu.VMEM((ts, vd), BF16)],
        compiler_params=_params(),
        name="delta_rule",
    )(q, k, v, gates, z, h, norm_g.reshape(1, LANES), w_out.astype(BF16))


def kernel(x, norm_mix_g, norm_ffn_g, final_norm_g, cv_w_pw1, cv_b_pw1, cv_w_dw, cv_b_dw, cv_ln_g, cv_ln_b, cv_w_pw2, cv_b_pw2, gdn_w_in, gdn_conv_w, gdn_a_log, gdn_dt_bias, gdn_norm_g, gdn_w_out, mlp_w1, mlp_w2):
    depth = norm_mix_g.shape[0]
    h = x
    for i in range(depth):
        j = i // 2
        last = i == depth - 1
        if i % 2 == 0:
            h = _conv_layer(h, norm_mix_g[i], cv_w_pw1[j], cv_b_pw1[j], cv_w_dw[j], cv_b_dw[j],
                            cv_ln_g[j], cv_ln_b[j], cv_w_pw2[j], cv_b_pw2[j],
                            norm_ffn_g[i], mlp_w1[i], mlp_w2[i], final_norm_g, last)
        else:
            h = _gdn_mixer(h, norm_mix_g[i], gdn_w_in[j], gdn_conv_w[j], gdn_a_log[j],
                           gdn_dt_bias[j], gdn_norm_g[j], gdn_w_out[j])
            h = _mlp(h, norm_ffn_g[i], mlp_w1[i], mlp_w2[i], final_norm_g, last)
    return h
```

```python
import functools

import jax
import jax.numpy as jnp
from jax import lax
from jax.experimental import pallas as pl
from jax.experimental.pallas import tpu as pltpu

F32 = jnp.float32
BF16 = jnp.bfloat16
NORM_EPS = 1e-6
L2_EPS = 1e-6
CHUNK = 64
INV_BLOCK = 16
LANES = 128
SUBLANES = 8
VMEM_LIMIT_BYTES = 56 * 1024 * 1024

CONV_TILE = 512
MLP_TILE = 512
GDN_TILE = 512
DELTA_TILE = 512
DELTA_GROUP = 4
CONV_HALO = 32
SHORT_HALO = 8
CONV_ROWS = 32
FF_CHUNK = 1024


def _rms_norm(x, g):
    ms = jnp.mean(x * x, axis=-1, keepdims=True)
    return x * lax.rsqrt(ms + NORM_EPS) * g


def _silu(x):
    return x * jax.nn.sigmoid(x)


def _dot(a, b):
    return jnp.dot(a, b, preferred_element_type=F32)


def _full(shape):
    return pl.BlockSpec(shape, lambda b, s: (0,) * len(shape))


def _tile(ts, d):
    return pl.BlockSpec((None, ts, d), lambda b, s: (b, s, 0))


def _params():
    return pltpu.CompilerParams(
        dimension_semantics=("arbitrary", "arbitrary"),
        vmem_limit_bytes=VMEM_LIMIT_BYTES)


def _run_interleaved(*gens):
    live = list(gens)
    while live:
        for g in list(live):
            try:
                next(g)
            except StopIteration:
                live.remove(g)


def _alternate(*gens):
    live = list(gens)
    while live:
        for g in list(live):
            try:
                next(g)
            except StopIteration:
                live.remove(g)
                continue
            yield


def _mlp_kernel(h_ref, g_ref, w1_ref, w2_ref, fg_ref, o_ref, *, final_norm):
    d_ff = w1_ref.shape[1]
    hn = _rms_norm(h_ref[...], g_ref[...]).astype(BF16)
    acc = None
    for j in range(d_ff // FF_CHUNK):
        cols = slice(j * FF_CHUNK, (j + 1) * FF_CHUNK)
        a = jnp.maximum(_dot(hn, w1_ref[:, cols]), 0.0)
        p = _dot((a * a).astype(BF16), w2_ref[cols, :])
        acc = p if acc is None else acc + p
    y = h_ref[...] + acc
    if final_norm:
        y = _rms_norm(y, fg_ref[...])
    o_ref[...] = y


def _mlp(h, g, w1, w2, final_g, final_norm):
    bsz, seq, d = h.shape
    d_ff = w1.shape[1]
    ts = min(MLP_TILE, seq)
    return pl.pallas_call(
        functools.partial(_mlp_kernel, final_norm=final_norm),
        out_shape=jax.ShapeDtypeStruct(h.shape, F32),
        grid=(bsz, seq // ts),
        in_specs=[_tile(ts, d), _full((1, d)), _full((d, d_ff)), _full((d_ff, d)), _full((1, d))],
        out_specs=_tile(ts, d),
        compiler_params=_params(),
        name="mlp_final" if final_norm else "mlp",
    )(h, g.reshape(1, d), w1.astype(BF16), w2.astype(BF16), final_g.reshape(1, d))


def _conv_layer_kernel(x_ref, g_ref, w1_ref, b1_ref, wdw_ref, bdw_ref, lng_ref, lnb_ref,
                       w2_ref, b2_ref, mg_ref, mw1_ref, mw2_ref, fg_ref, o_ref, *scratch,
                       final_norm, tiles_per_row):
    t = pl.program_id(0)
    ts, d = x_ref.shape
    slabs, taps = wdw_ref.shape[0], wdw_ref.shape[1]
    d_ff = mw1_ref.shape[1]
    ubufs, cvos = scratch[:slabs], scratch[slabs:2 * slabs]
    act, h1 = scratch[2 * slabs], scratch[2 * slabs + 1]
    base = CONV_HALO - (taps - 1)
    stride = CONV_ROWS // SUBLANES
    slot = t % 2

    @pl.when(t % tiles_per_row == 0)
    def _():
        for lt in range(slabs):
            ubufs[lt][0:CONV_HALO, :] = jnp.zeros((CONV_HALO, LANES), F32)

    @pl.when(t == 0)
    def _():
        h1[1] = jnp.zeros((ts, d), F32)

    def mixer():
        hn = _rms_norm(x_ref[...], g_ref[...]).astype(BF16)
        for lt in range(slabs):
            cols = slice(2 * lt * LANES, 2 * (lt + 1) * LANES)
            u = _dot(hn, w1_ref[:, cols]) + b1_ref[:, cols]
            yield
            ubuf, cvo = ubufs[lt], cvos[lt]
            ubuf[CONV_HALO:CONV_HALO + ts, :] = u[:, :LANES] * jax.nn.sigmoid(u[:, LANES:])
            w = [wdw_ref[lt, k] for k in range(taps)]
            bias = bdw_ref[lt]
            for r0 in range(0, ts, CONV_ROWS):
                acc = [bias] * stride
                for m in range(taps + stride - 1):
                    win = ubuf[pl.ds(r0 + base + m, SUBLANES, stride=stride), :]
                    for j in range(stride):
                        if 0 <= m - j < taps:
                            acc[j] = acc[j] + w[m - j] * win
                for j in range(stride):
                    cvo[pl.ds(r0 + j, SUBLANES, stride=stride), :] = acc[j]
            ubuf[0:CONV_HALO, :] = ubuf[ts:ts + CONV_HALO, :]
        for r0 in range(0, ts, CONV_ROWS):
            if r0 % (ts // 4) == 0:
                yield
            xs = [cvos[lt][r0:r0 + CONV_ROWS, :] for lt in range(slabs)]
            tot = xs[0]
            for lt in range(1, slabs):
                tot = tot + xs[lt]
            mu = jnp.sum(tot, axis=-1, keepdims=True) * (1.0 / d)
            xc = [x - mu for x in xs]
            sq = xc[0] * xc[0]
            for lt in range(1, slabs):
                sq = sq + xc[lt] * xc[lt]
            inv = lax.rsqrt(jnp.sum(sq, axis=-1, keepdims=True) * (1.0 / d) + NORM_EPS)
            for lt in range(slabs):
                cols = slice(lt * LANES, (lt + 1) * LANES)
                y = xc[lt] * inv * lng_ref[:, cols] + lnb_ref[:, cols]
                act[r0:r0 + CONV_ROWS, cols] = _silu(y).astype(BF16)
        yield
        h1[slot] = x_ref[...] + _dot(act[...], w2_ref[...]) + b2_ref[...]

    def mlp():
        hn = _rms_norm(h1[1 - slot], mg_ref[...]).astype(BF16)
        acc = None
        for j in range(d_ff // FF_CHUNK):
            cols = slice(j * FF_CHUNK, (j + 1) * FF_CHUNK)
            a = jnp.maximum(_dot(hn, mw1_ref[:, cols]), 0.0)
            yield
            p = _dot((a * a).astype(BF16), mw2_ref[cols, :])
            acc = p if acc is None else acc + p
            yield
        y = h1[1 - slot] + acc
        if final_norm:
            y = _rms_norm(y, fg_ref[...])
        o_ref[...] = y

    _run_interleaved(mixer(), mlp())


def _conv_layer(h, g, w1, b1, wdw, bdw, lng, lnb, w2, b2, mg, mw1, mw2, final_g, final_norm):
    bsz, seq, d = h.shape
    ts = min(CONV_TILE, seq)
    n_tiles = seq // ts
    taps = wdw.shape[0]
    slabs = d // LANES
    d_ff = mw1.shape[1]
    wdw_b = jnp.broadcast_to(wdw.reshape(taps, slabs, 1, LANES).transpose(1, 0, 2, 3),
                             (slabs, taps, SUBLANES, LANES))
    bdw_b = jnp.broadcast_to(bdw.reshape(slabs, 1, LANES), (slabs, SUBLANES, LANES))
    w1_g = w1.reshape(d, 2, slabs, LANES).transpose(0, 2, 1, 3).reshape(d, 2 * d).astype(BF16)
    b1_g = b1.reshape(2, slabs, LANES).transpose(1, 0, 2).reshape(1, 2 * d)

    def const(shape):
        return pl.BlockSpec(shape, lambda t: (0,) * len(shape), pipeline_mode=pl.Buffered(1))

    total = bsz * n_tiles
    out = pl.pallas_call(
        functools.partial(_conv_layer_kernel, final_norm=final_norm, tiles_per_row=n_tiles),
        out_shape=jax.ShapeDtypeStruct((total, ts, d), F32),
        grid=(total + 1,),
        in_specs=[pl.BlockSpec((None, ts, d), lambda t: (jnp.minimum(t, total - 1), 0, 0)),
                  const((1, d)), const((d, 2 * d)), const((1, 2 * d)),
                  const((slabs, taps, SUBLANES, LANES)), const((slabs, SUBLANES, LANES)),
                  const((1, d)), const((1, d)), const((d, d)), const((1, d)),
                  const((1, d)), const((d, d_ff)), const((d_ff, d)), const((1, d))],
        out_specs=pl.BlockSpec((None, ts, d), lambda t: (jnp.maximum(t - 1, 0), 0, 0)),
        scratch_shapes=([pltpu.VMEM((CONV_HALO + ts, LANES), F32)] * slabs
                        + [pltpu.VMEM((ts, LANES), F32)] * slabs
                        + [pltpu.VMEM((ts, d), BF16), pltpu.VMEM((2, ts, d), F32)]),
        compiler_params=pltpu.CompilerParams(dimension_semantics=("arbitrary",),
                                             vmem_limit_bytes=VMEM_LIMIT_BYTES),
        name="conv_layer",
    )(h.reshape(total, ts, d), g.reshape(1, d), w1_g, b1_g, wdw_b, bdw_b, lng.reshape(1, d),
      lnb.reshape(1, d), w2.astype(BF16), b2.reshape(1, d), mg.reshape(1, d),
      mw1.astype(BF16), mw2.astype(BF16), final_g.reshape(1, d))
    return out.reshape(bsz, seq, d)


def _gdn_in_kernel(h_ref, g_ref, wq_ref, wab_ref, cw_ref, alog_ref, dtb_ref,
                   q_ref, k_ref, v_ref, z_ref, gt_ref, *cbufs, heads):
    s = pl.program_id(1)
    ts, d = h_ref.shape
    slabs, taps = cw_ref.shape[0], cw_ref.shape[1]
    qkv_dim = slabs * LANES
    base = SHORT_HALO - (taps - 1)
    stride = CONV_ROWS // SUBLANES
    q_scale = LANES ** -0.5

    @pl.when(s == 0)
    def _():
        for cbuf in cbufs:
            cbuf[0:SHORT_HALO, :] = jnp.zeros((SHORT_HALO, LANES), F32)

    hn = _rms_norm(h_ref[...], g_ref[...]).astype(BF16)

    ab = _dot(hn, wab_ref[...])
    pre = ab + dtb_ref[...]
    softplus = jnp.maximum(pre, 0.0) + jnp.log1p(jnp.exp(-jnp.abs(pre)))
    lane = lax.broadcasted_iota(jnp.int32, ab.shape, 1)
    gt_ref[...] = jnp.where(lane < heads, -jnp.exp(alog_ref[...]) * softplus, jax.nn.sigmoid(ab))

    for blk in range(slabs // 2):
        proj = _dot(hn, wq_ref[:, 2 * blk * LANES:2 * (blk + 1) * LANES])
        for sl in (2 * blk, 2 * blk + 1):
            cbuf = cbufs[sl]
            cbuf[SHORT_HALO:SHORT_HALO + ts, :] = proj[:, (sl % 2) * LANES:(sl % 2 + 1) * LANES]
            w = [cw_ref[sl, t] for t in range(taps)]
            for r0 in range(0, ts, CONV_ROWS):
                win = [cbuf[pl.ds(r0 + base + m, SUBLANES, stride=stride), :]
                       for m in range(taps + stride - 1)]
                for j in range(stride):
                    acc = w[0] * win[j]
                    for t in range(1, taps):
                        acc = acc + w[t] * win[j + t]
                    y = _silu(acc)
                    dst = pl.ds(r0 + j, SUBLANES, stride=stride)
                    if sl < heads:
                        q_ref[sl, dst, :] = (
                            y * lax.rsqrt(jnp.sum(y * y, axis=-1, keepdims=True) + L2_EPS) * q_scale)
                    elif sl < 2 * heads:
                        k_ref[sl - heads, dst, :] = (
                            y * lax.rsqrt(jnp.sum(y * y, axis=-1, keepdims=True) + L2_EPS))
                    else:
                        v_ref[sl - 2 * heads, dst, :] = y
            cbuf[0:SHORT_HALO, :] = cbuf[ts:ts + SHORT_HALO, :]
    z_ref[...] = _dot(hn, wq_ref[:, qkv_dim:])


def _delta_kernel(q_ref, k_ref, v_ref, gt_ref, z_ref, h_ref, ng_ref, wo_ref, out_ref,
                  state, o_buf, act, *, heads):
    s = pl.program_id(1)
    ts = q_ref.shape[1]
    dv = v_ref.shape[2]
    c = CHUNK
    n_chunks = ts // c

    @pl.when(s == 0)
    def _():
        state[...] = jnp.zeros(state.shape, F32)

    row = lax.broadcasted_iota(jnp.int32, (c, 2 * c), 0)
    lane = lax.broadcasted_iota(jnp.int32, (c, 2 * c), 1)
    col = jnp.where(lane < c, lane, lane - c)
    low = lane < c
    low_row = low[:1]
    low_tall = lax.broadcasted_iota(jnp.int32, (2 * c, 2 * c), 1) < c
    causal = row >= col
    eye = (col == row).astype(F32)
    diag_block = (row > col) & (row // INV_BLOCK == col // INV_BLOCK)
    merge_sizes = []
    b = INV_BLOCK
    while b < c:
        merge_sizes.append(b)
        b *= 2
    merge_masks = [(row // (2 * b) == col // (2 * b)) & (row // b == col // b + 1)
                   for b in merge_sizes]
    n_rounds = (INV_BLOCK - 1).bit_length()
    tril = (lax.broadcasted_iota(jnp.int32, (c, c), 0)
            >= lax.broadcasted_iota(jnp.int32, (c, c), 1)).astype(F32)
    sel = (lax.broadcasted_iota(jnp.int32, (heads, LANES), 0)
           == lax.broadcasted_iota(jnp.int32, (heads, LANES), 1)).astype(F32)
    prepared = {}

    def side_by_side(a, b):
        return jnp.concatenate([a, b], axis=1)

    def block_diag(a, b):
        return jnp.concatenate([side_by_side(a, jnp.zeros_like(b)),
                                side_by_side(jnp.zeros_like(a), b)], axis=0)

    def packed_block_diag(x):
        keep_a = low if x.shape[0] == c else low_tall
        zero = jnp.zeros_like(x)
        return jnp.concatenate([jnp.where(keep_a, x, zero), jnp.where(keep_a, zero, x)], axis=0)

    pairs = [(h, h + 1) for h in range(0, heads, 2)]

    def intra(chunks):
        gates, gcum, gcum_t = {}, {}, {}
        for ci in chunks:
            gates[ci] = gt_ref[ci * c:(ci + 1) * c, :]
            gcum[ci] = jnp.dot(tril, gates[ci], preferred_element_type=F32,
                               precision=lax.Precision.HIGHEST)
            gcum_t[ci] = lax.dot_general(
                sel, jnp.concatenate([gcum[ci], gcum[ci]], axis=0), (((1,), (1,)), ((), ())),
                preferred_element_type=F32, precision=lax.Precision.HIGHEST)
        yield
        duos = [(ci, ha, hb) for ci in chunks for ha, hb in pairs]
        x1, decay, kdec_t16, rhs16, qg16, g_last = {}, {}, {}, {}, {}, {}
        for duo in duos:
            ci, ha, hb = duo
            rows = slice(ci * c, (ci + 1) * c)
            lhs, kh_pair = [], []
            for h in (ha, hb):
                g_col = gcum[ci][:, h:h + 1]
                beta = gates[ci][:, heads + h:heads + h + 1]
                qh = q_ref[h, rows, :]
                kh = k_ref[h, rows, :]
                vh = v_ref[h, rows, :]
                kb = kh * beta
                e_col = jnp.exp(g_col)
                lhs.append(jnp.concatenate([kb.astype(BF16), qh.astype(BF16)], axis=0))
                kh_pair.append(kh)
                rhs16[ci, h] = jnp.concatenate([vh * beta, kb * e_col], axis=-1).astype(BF16)
                qg16[ci, h] = (qh * e_col).astype(BF16)
                g_last[ci, h] = gcum[ci][c - 1:c, h:h + 1]
            g_col = jnp.where(low, gcum[ci][:, ha:ha + 1], gcum[ci][:, hb:hb + 1])
            g_row = jnp.where(low_row, gcum_t[ci][ha:ha + 1, :], gcum_t[ci][hb:hb + 1, :])
            g_end = jnp.where(low_row, g_last[ci, ha], g_last[ci, hb])
            decay[duo] = jnp.exp(jnp.where(causal, g_col - g_row, -jnp.inf))
            k_t = jnp.concatenate(kh_pair, axis=0).T
            kdec_t16[duo] = (k_t * jnp.exp(g_end - g_row)).astype(BF16)
            x1[duo] = _dot(side_by_side(*lhs), packed_block_diag(k_t.astype(BF16)))
        yield
        inv, power, lower16, qk16 = {}, {}, {}, {}
        for duo in duos:
            lower = x1[duo][:c] * decay[duo]
            lower16[duo] = lower.astype(BF16)
            qk16[duo] = (x1[duo][c:] * decay[duo]).astype(BF16)
            inv[duo] = eye
            power[duo] = jnp.where(diag_block, -lower, 0.0)
        for r in range(n_rounds):
            prod = {}
            for duo in duos:
                rhs = packed_block_diag(inv[duo].astype(BF16))
                if r + 1 < n_rounds:
                    rhs = side_by_side(rhs, packed_block_diag(power[duo].astype(BF16)))
                prod[duo] = _dot(power[duo].astype(BF16), rhs)
            yield
            for duo in duos:
                inv[duo] = inv[duo] + prod[duo][:, :2 * c]
                if r + 1 < n_rounds:
                    power[duo] = prod[duo][:, 2 * c:]
        for mask in merge_masks:
            tmp = {}
            for duo in duos:
                corner16 = jnp.where(mask, lower16[duo], jnp.zeros_like(lower16[duo]))
                tmp[duo] = _dot(corner16, packed_block_diag(inv[duo].astype(BF16)))
            yield
            upd = {}
            for duo in duos:
                upd[duo] = _dot(inv[duo].astype(BF16), packed_block_diag(tmp[duo].astype(BF16)))
            yield
            inv = {duo: inv[duo] - upd[duo] for duo in duos}
        sol = {}
        for duo in duos:
            ci, ha, hb = duo
            sol[duo] = _dot(inv[duo].astype(BF16), block_diag(rhs16[ci, ha], rhs16[ci, hb]))
        yield
        for duo in duos:
            ci, ha, hb = duo
            width = sol[duo].shape[1] // 2
            halves = (sol[duo][:, :width], sol[duo][:, width:])
            prepared[duo] = dict(
                u=[x[:, :dv] for x in halves],
                lhs_proj=side_by_side(*[
                    jnp.concatenate([x[:, dv:].astype(BF16), qg16[ci, h]], axis=0)
                    for x, h in zip(halves, (ha, hb))]),
                lhs_intra=jnp.concatenate([qk16[duo], kdec_t16[duo]], axis=0),
                g_last=[g_last[ci, ha], g_last[ci, hb]])

    def scan(ci):
        rows = slice(ci * c, (ci + 1) * c)
        p = [prepared.pop((ci, ha, hb)) for ha, hb in pairs]
        proj = []
        for (ha, hb), pp in zip(pairs, p):
            proj.append(_dot(pp["lhs_proj"],
                             block_diag(state[ha].astype(BF16), state[hb].astype(BF16))))
        yield
        intra_out = []
        for pp, pj in zip(p, proj):
            va16 = (pp["u"][0] - pj[:c, :dv]).astype(BF16)
            vb16 = (pp["u"][1] - pj[:c, dv:]).astype(BF16)
            intra_out.append(_dot(pp["lhs_intra"], block_diag(va16, vb16)))
        yield
        for (ha, hb), pp, pj, io in zip(pairs, p, proj, intra_out):
            o_buf[rows, ha * dv:(hb + 1) * dv] = pj[c:] + io[:c]
            state[ha] = state[ha] * jnp.exp(pp["g_last"][0]) + io[c:, :dv]
            state[hb] = state[hb] * jnp.exp(pp["g_last"][1]) + io[c:, dv:]
        yield

    def scans(chunks):
        for ci in chunks:
            yield from scan(ci)

    def delta_rule():
        groups = [list(range(i, min(i + DELTA_GROUP, n_chunks)))
                  for i in range(0, n_chunks, DELTA_GROUP)]
        yield from intra(groups[0])
        for gi, group in enumerate(groups):
            if gi + 1 < len(groups):
                yield from _alternate(scans(group), intra(groups[gi + 1]))
            else:
                yield from scans(group)

    _run_interleaved(delta_rule())
    for h in range(heads):
        cols = slice(h * dv, (h + 1) * dv)
        oh = o_buf[:, cols]
        on = oh * lax.rsqrt(jnp.mean(oh * oh, axis=-1, keepdims=True) + NORM_EPS) * ng_ref[...]
        act[:, cols] = (on * _silu(z_ref[:, cols])).astype(BF16)
    out_ref[...] = h_ref[...] + _dot(act[...], wo_ref[...])


def _gdn_mixer(h, g, w_in, conv_w, a_log, dt_bias, norm_g, w_out):
    bsz, seq, d = h.shape
    heads = a_log.shape[0]
    taps, qkv_dim = conv_w.shape
    vd = w_out.shape[0]
    slabs = qkv_dim // LANES
    assert qkv_dim == 3 * vd and vd == heads * LANES, (qkv_dim, vd, heads)

    w_qkvz = w_in[:, :qkv_dim + vd].astype(BF16)
    w_ab = jnp.pad(w_in[:, qkv_dim + vd:], ((0, 0), (0, LANES - 2 * heads))).astype(BF16)
    alog = jnp.pad(a_log, (0, LANES - heads)).reshape(1, LANES)
    dtb = jnp.pad(dt_bias, (0, LANES - heads)).reshape(1, LANES)
    cw_b = jnp.broadcast_to(conv_w.reshape(taps, slabs, 1, LANES).transpose(1, 0, 2, 3),
                            (slabs, taps, SUBLANES, LANES))

    def head_major(ts):
        return pl.BlockSpec((None, heads, ts, LANES), lambda b, s: (b, 0, s, 0))

    ts = min(GDN_TILE, seq)
    hm_shape = jax.ShapeDtypeStruct((bsz, heads, seq, LANES), F32)
    q, k, v, z, gates = pl.pallas_call(
        functools.partial(_gdn_in_kernel, heads=heads),
        out_shape=[hm_shape, hm_shape, hm_shape,
                   jax.ShapeDtypeStruct((bsz, seq, vd), F32),
                   jax.ShapeDtypeStruct((bsz, seq, LANES), F32)],
        grid=(bsz, seq // ts),
        in_specs=[_tile(ts, d), _full((1, d)), _full((d, qkv_dim + vd)), _full((d, LANES)),
                  _full((slabs, taps, SUBLANES, LANES)), _full((1, LANES)), _full((1, LANES))],
        out_specs=[head_major(ts), head_major(ts), head_major(ts), _tile(ts, vd), _tile(ts, LANES)],
        scratch_shapes=[pltpu.VMEM((SHORT_HALO + ts, LANES), F32)] * slabs,
        compiler_params=_params(),
        name="gdn_in",
    )(h, g.reshape(1, d), w_qkvz, w_ab, cw_b, alog, dtb)

    ts = min(DELTA_TILE, seq)
    return pl.pallas_call(
        functools.partial(_delta_kernel, heads=heads),
        out_shape=jax.ShapeDtypeStruct(h.shape, F32),
        grid=(bsz, seq // ts),
        in_specs=[head_major(ts), head_major(ts), head_major(ts), _tile(ts, LANES),
                  _tile(ts, vd), _tile(ts, d), _full((1, LANES)), _full((vd, d))],
        out_specs=_tile(ts, d),
        scratch_shapes=[pltpu.VMEM((heads, LANES, LANES), F32),
                        pltpu.VMEM((ts, vd), F32),
                        pltpu.VMEM((ts, vd), BF16)],
        compiler_params=_params(),
        name="delta_rule",
    )(q, k, v, gates, z, h, norm_g.reshape(1, LANES), w_out.astype(BF16))


def kernel(x, norm_mix_g, norm_ffn_g, final_norm_g, cv_w_pw1, cv_b_pw1, cv_w_dw, cv_b_dw, cv_ln_g, cv_ln_b, cv_w_pw2, cv_b_pw2, gdn_w_in, gdn_conv_w, gdn_a_log, gdn_dt_bias, gdn_norm_g, gdn_w_out, mlp_w1, mlp_w2):
    depth = norm_mix_g.shape[0]
    h = x
    for i in range(depth):
        j = i // 2
        last = i == depth - 1
        if i % 2 == 0:
            h = _conv_layer(h, norm_mix_g[i], cv_w_pw1[j], cv_b_pw1[j], cv_w_dw[j], cv_b_dw[j],
                            cv_ln_g[j], cv_ln_b[j], cv_w_pw2[j], cv_b_pw2[j],
                            norm_ffn_g[i], mlp_w1[i], mlp_w2[i], final_norm_g, last)
        else:
            h = _gdn_mixer(h, norm_mix_g[i], gdn_w_in[j], gdn_conv_w[j], gdn_a_log[j],
                           gdn_dt_bias[j], gdn_norm_g[j], gdn_w_out[j])
            h = _mlp(h, norm_ffn_g[i], mlp_w1[i], mlp_w2[i], final_norm_g, last)
    return h
```
